```python
import jax, jax.numpy as jnp
from jax import lax
import numpy as np

D_MODEL = 2048
BATCH = 1
SEQ = 8192
DEPTH = 1

D_MIX = 2 * D_MODEL
D_A = D_MIX // 2
CHUNK_A = 128
N_GROUPS_A = 16
D_HEAD_A = D_A // N_GROUPS_A
D_SSM = D_MIX - D_A
SSM_HEAD_DIM = 64
N_SSM_HEADS = D_SSM // SSM_HEAD_DIM
N_SSM_GROUPS = 8
HEADS_PER_GROUP = N_SSM_HEADS // N_SSM_GROUPS
D_STATE = 128
SSM_CONV = 4
SSD_CHUNK = 128
D_XBC = D_SSM + 2 * N_SSM_GROUPS * D_STATE
D_IN = 2 * D_A + D_SSM + D_XBC + N_SSM_HEADS
D_FF = 5632
FFN_CONV = 3
D_PLE = 256
EPS = 1e-6

kernel_name = "hybrid_gmlp_ssd_convffn_ple"


def rms_norm(x, g):
    xf = x.astype(jnp.float32)
    y = xf * lax.rsqrt(jnp.mean(xf * xf, axis=-1, keepdims=True) + EPS)
    return (y * g.astype(jnp.float32)).astype(x.dtype)


def causal_dwconv(x, w, b):
    k_w = w.shape[0]
    s = x.shape[1]
    xp = jnp.pad(x, ((0, 0), (k_w - 1, 0), (0, 0)))
    y = b
    for k in range(k_w):
        y = y + xp[:, k:k + s] * w[k]
    return y


def gmlp_chunk_mixer(uv, ln_g, ln_b, w_s, b_s):
    bn, s, _ = uv.shape
    uv = jax.nn.gelu(uv)
    u, v = jnp.split(uv, 2, axis=-1)
    v = v.reshape(bn, s // CHUNK_A, CHUNK_A, N_GROUPS_A, D_HEAD_A)
    vf = v.astype(jnp.float32)
    mu = jnp.mean(vf, axis=-1, keepdims=True)
    var = jnp.mean(jnp.square(vf - mu), axis=-1, keepdims=True)
    g = ln_g.reshape(N_GROUPS_A, D_HEAD_A).astype(jnp.float32)
    bb = ln_b.reshape(N_GROUPS_A, D_HEAD_A).astype(jnp.float32)
    vn = ((vf - mu) * lax.rsqrt(var + EPS) * g + bb).astype(uv.dtype)
    mask = jnp.tril(jnp.ones((CHUNK_A, CHUNK_A), dtype=bool))
    ws = jnp.where(mask[None], w_s, jnp.zeros_like(w_s))
    sg = jnp.einsum('gts,bcsgd->bctgd', ws, vn) + b_s.T[:, :, None]
    return u * sg.reshape(bn, s, D_A)


def ssd_chunked(xs, dt, a, bm, cm, d_skip):
    bn, s, _ = xs.shape
    nc = s // SSD_CHUNK
    shp = (bn, nc, SSD_CHUNK, N_SSM_GROUPS, HEADS_PER_GROUP)
    x5 = xs.reshape(shp + (SSM_HEAD_DIM,))
    dt5 = dt.reshape(shp)
    b5 = bm.reshape(bn, nc, SSD_CHUNK, N_SSM_GROUPS, D_STATE)
    c5 = cm.reshape(bn, nc, SSD_CHUNK, N_SSM_GROUPS, D_STATE)
    a_dt = dt5 * a.reshape(N_SSM_GROUPS, HEADS_PER_GROUP)
    x_dt = x5 * dt5[..., None]
    a_cs = jnp.cumsum(a_dt, axis=2)
    a_t = jnp.moveaxis(a_cs, 2, -1)
    diff = a_t[..., :, None] - a_t[..., None, :]
    mask = jnp.tril(jnp.ones((SSD_CHUNK, SSD_CHUNK), dtype=bool))
    l_mat = jnp.exp(jnp.where(mask, diff, -jnp.inf))
    scores = jnp.einsum('bclgn,bcsgn->bcgls', c5, b5)
    y_diag = jnp.einsum('bcgrls,bcsgrp->bclgrp', scores[:, :, :, None] * l_mat, x_dt)
    decay_states = jnp.exp(a_cs[:, :, -1:] - a_cs)
    states = jnp.einsum('bclgn,bclgrp->bcgrpn', b5, x_dt * decay_states[..., None])
    chunk_decay = jnp.exp(a_cs[:, :, -1])

    def step(h, inp):
        dec, st = inp
        return dec[..., None, None] * h + st, h

    h0 = jnp.zeros((bn, N_SSM_GROUPS, HEADS_PER_GROUP, SSM_HEAD_DIM, D_STATE), jnp.float32)
    _, h_in = lax.scan(step, h0, (jnp.moveaxis(chunk_decay, 1, 0), jnp.moveaxis(states, 1, 0)))
    h_in = jnp.moveaxis(h_in, 0, 1)
    y_off = jnp.einsum('bclgn,bcgrpn->bclgrp', c5, h_in) * jnp.exp(a_cs)[..., None]
    y = y_diag + y_off + d_skip.reshape(N_SSM_GROUPS, HEADS_PER_GROUP, 1) * x5
    return y.reshape(bn, s, D_SSM)


def mamba2_mixer(z, xbc, dt_raw, conv_w, conv_b, dt_bias, a_log, d_skip, norm_g):
    bn, s, _ = z.shape
    xbc = jax.nn.silu(causal_dwconv(xbc, conv_w, conv_b))
    n_bc = N_SSM_GROUPS * D_STATE
    xs = xbc[..., :D_SSM].astype(jnp.float32)
    bm = xbc[..., D_SSM:D_SSM + n_bc].astype(jnp.float32)
    cm = xbc[..., D_SSM + n_bc:].astype(jnp.float32)
    dt = jax.nn.softplus(dt_raw.astype(jnp.float32) + dt_bias.astype(jnp.float32))
    a = -jnp.exp(a_log.astype(jnp.float32))
    y = ssd_chunked(xs, dt, a, bm, cm, d_skip.astype(jnp.float32))
    y = y * jax.nn.silu(z.astype(jnp.float32))
    yg = y.reshape(bn, s, N_SSM_GROUPS, D_SSM // N_SSM_GROUPS)
    yg = yg * lax.rsqrt(jnp.mean(yg * yg, axis=-1, keepdims=True) + EPS)
    y = yg.reshape(bn, s, D_SSM) * norm_g.astype(jnp.float32)
    return y.astype(z.dtype)


def setup_inputs(seed: int = 0) -> dict:
    key = jax.random.key(seed)
    ks = jax.random.split(key, 32)
    f32 = jnp.float32
    nrm = lambda k, shape, scale: jax.random.normal(k, shape, f32) * scale
    gain = lambda k, n: 1.0 + 0.05 * jax.random.normal(k, (DEPTH, n), f32)
    dt0 = jnp.exp(jax.random.uniform(ks[5], (DEPTH, N_SSM_HEADS), f32,
                                     np.log(1e-3).astype(np.float32), np.log(1e-1).astype(np.float32)))
    dt_bias = dt0 + jnp.log(-jnp.expm1(-dt0))
    return {
        "x": jax.random.normal(ks[0], (BATCH, SEQ, D_MODEL), f32),
        "p": jax.random.normal(ks[1], (DEPTH, BATCH, SEQ, D_PLE), f32),
        "norm_mix_g": gain(ks[2], D_MODEL),
        "w_in": nrm(ks[3], (DEPTH, D_MODEL, D_IN), D_MODEL ** -0.5),
        "ln_a_g": gain(ks[4], D_A),
        "ln_a_b": nrm(ks[6], (DEPTH, D_A), 0.02),
        "w_s": nrm(ks[7], (DEPTH, N_GROUPS_A, CHUNK_A, CHUNK_A), CHUNK_A ** -0.5),
        "b_s": 1.0 + nrm(ks[8], (DEPTH, N_GROUPS_A, CHUNK_A), 0.1),
        "norm_a_g": gain(ks[9], D_A),
        "conv_ssm_w": nrm(ks[10], (DEPTH, SSM_CONV, D_XBC), SSM_CONV ** -0.5),
        "conv_ssm_b": nrm(ks[11], (DEPTH, D_XBC), 0.02),
        "dt_bias": dt_bias,
        "a_log": jnp.log(jax.random.uniform(ks[12], (DEPTH, N_SSM_HEADS), f32, 1.0, 16.0)),
        "d_skip": 1.0 + nrm(ks[13], (DEPTH, N_SSM_HEADS), 0.1),
        "ssm_norm_g": gain(ks[14], D_SSM),
        "w_out": nrm(ks[15], (DEPTH, D_MIX, D_MODEL), D_MIX ** -0.5),
        "norm_ffn_g": gain(ks[16], D_MODEL),
        "w_up": nrm(ks[17], (DEPTH, D_MODEL, 2 * D_FF), D_MODEL ** -0.5),
        "conv_ffn_w": nrm(ks[18], (DEPTH, FFN_CONV, 2 * D_FF), FFN_CONV ** -0.5),
        "conv_ffn_b": nrm(ks[19], (DEPTH, 2 * D_FF), 0.02),
        "w_down": nrm(ks[20], (DEPTH, D_FF, D_MODEL), D_FF ** -0.5),
        "norm_ple_g": gain(ks[21], D_MODEL),
        "w_ple_gate": nrm(ks[22], (DEPTH, D_MODEL, D_MODEL), D_MODEL ** -0.5),
        "w_ple": nrm(ks[23], (DEPTH, D_PLE, D_MODEL), D_PLE ** -0.5),
        "norm_final_g": 1.0 + 0.05 * jax.random.normal(ks[24], (D_MODEL,), f32),
    }


def reference(x, p, norm_mix_g, w_in, ln_a_g, ln_a_b, w_s, b_s, norm_a_g,
              conv_ssm_w, conv_ssm_b, dt_bias, a_log, d_skip, ssm_norm_g, w_out,
              norm_ffn_g, w_up, conv_ffn_w, conv_ffn_b, w_down,
              norm_ple_g, w_ple_gate, w_ple, norm_final_g):
    h = x
    o_z = 2 * D_A
    o_xbc = o_z + D_SSM
    o_dt = o_xbc + D_XBC
    for i in range(DEPTH):
        a = rms_norm(h, norm_mix_g[i])
        proj = a @ w_in[i]
        y_a = gmlp_chunk_mixer(proj[..., :o_z], ln_a_g[i], ln_a_b[i], w_s[i], b_s[i])
        y_a = rms_norm(y_a, norm_a_g[i])
        y_b = mamba2_mixer(proj[..., o_z:o_xbc], proj[..., o_xbc:o_dt], proj[..., o_dt:],
                           conv_ssm_w[i], conv_ssm_b[i], dt_bias[i], a_log[i], d_skip[i],
                           ssm_norm_g[i])
        h = h + jnp.concatenate([y_a, y_b], axis=-1) @ w_out[i]
        f = rms_norm(h, norm_ffn_g[i])
        hid = causal_dwconv(f @ w_up[i], conv_ffn_w[i], conv_ffn_b[i])
        gate, up = jnp.split(hid, 2, axis=-1)
        h = h + (jax.nn.silu(gate) * up) @ w_down[i]
        g_ple = jax.nn.sigmoid(rms_norm(h, norm_ple_g[i]) @ w_ple_gate[i])
        h = h + g_ple * (p[i] @ w_ple[i])
    return rms_norm(h, norm_final_g)
```

```python
import functools

import jax
import jax.numpy as jnp
from jax import lax
from jax.experimental import pallas as pl
from jax.experimental.pallas import tpu as pltpu

F32 = jnp.float32
BF16 = jnp.bfloat16

D_MODEL = 2048
SEQ = 8192
D_A = 2048
CHUNK = 128
N_GROUPS_A = 16
D_HEAD_A = 128
D_SSM = 2048
SSM_HEAD_DIM = 64
N_SSM_HEADS = 32
N_SSM_GROUPS = 8
HEADS_PER_GROUP = 4
D_STATE = 128
SSM_CONV = 4
D_XBC = 4096
D_PROJ = 10240
D_FF = 5632
FFN_CONV = 3
D_PLE = 256
EPS = 1e-6

LANES = 128
SUBLANES = 8
HALO = SUBLANES
VMEM_LIMIT_BYTES = 56 * 1024 * 1024

TM_IN = 1024
TN_IN = 512
N_UV_TILES = (2 * D_A) // TN_IN
TM_GMLP = 512
R_SSD = 256
TM_OUT = 512
TN_OUT = 512
TM_UP = 1024
TN_UP = 512
TM_DOWN = 512
TN_DOWN = 512
TM_PLE = 512


def _params(*sem):
    return pltpu.CompilerParams(dimension_semantics=sem, vmem_limit_bytes=VMEM_LIMIT_BYTES)


def _sigmoid(x):
    return 1.0 / (1.0 + jnp.exp(-x))


def _rms_scale(x):
    return lax.rsqrt(jnp.mean(x * x, axis=-1, keepdims=True) + EPS)


def _in_proj_kernel(x_ref, g_ref, w_ref, wdt_ref, proj_ref, dt_ref, a_scr):
    j = pl.program_id(1)

    @pl.when(j == 0)
    def _():
        x = x_ref[...]
        a = (x * _rms_scale(x) * g_ref[...]).astype(BF16)
        a_scr[...] = a
        dt_ref[...] = jnp.dot(a, wdt_ref[...], preferred_element_type=F32)

    y = jnp.dot(a_scr[...], w_ref[...], preferred_element_type=F32)

    @pl.when(j < N_UV_TILES)
    def _():
        proj_ref[...] = jax.nn.gelu(y).astype(BF16)

    @pl.when(j >= N_UV_TILES)
    def _():
        proj_ref[...] = y.astype(BF16)


def _in_proj(x, g, w, wdt):
    return pl.pallas_call(
        _in_proj_kernel,
        grid=(SEQ // TM_IN, D_PROJ // TN_IN),
        in_specs=[
            pl.BlockSpec((TM_IN, D_MODEL), lambda i, j: (i, 0)),
            pl.BlockSpec((1, D_MODEL), lambda i, j: (0, 0)),
            pl.BlockSpec((D_MODEL, TN_IN), lambda i, j: (0, j)),
            pl.BlockSpec((D_MODEL, LANES), lambda i, j: (0, 0)),
        ],
        out_specs=[
            pl.BlockSpec((TM_IN, TN_IN), lambda i, j: (i, j)),
            pl.BlockSpec((TM_IN, LANES), lambda i, j: (i, 0)),
        ],
        out_shape=[
            jax.ShapeDtypeStruct((SEQ, D_PROJ), BF16),
            jax.ShapeDtypeStruct((SEQ, LANES), F32),
        ],
        scratch_shapes=[pltpu.VMEM((TM_IN, D_MODEL), BF16)],
        compiler_params=_params("arbitrary", "arbitrary"),
        name="in_proj",
    )(x, g, w, wdt)


def _gmlp_kernel(u_ref, v_ref, lng_ref, lnb_ref, ws_ref, bs_ref, ng_ref, ya_ref, wsm_scr, acc_scr):
    @pl.when(pl.program_id(0) == 0)
    def _():
        row = lax.broadcasted_iota(jnp.int32, (CHUNK, CHUNK), 0)
        col = lax.broadcasted_iota(jnp.int32, (CHUNK, CHUNK), 1)
        for g in range(N_GROUPS_A):
            wsm_scr[g] = jnp.where(row >= col, ws_ref[g], 0.0).astype(BF16)

    def chunk_body(c, carry):
        r0 = pl.multiple_of(c * CHUNK, CHUNK)
        rows = pl.ds(r0, CHUNK)
        ssq = jnp.zeros((CHUNK, 1), F32)
        for g in range(N_GROUPS_A):
            cols = slice(g * D_HEAD_A, (g + 1) * D_HEAD_A)
            v = v_ref[rows, cols].astype(F32)
            mu = jnp.mean(v, axis=-1, keepdims=True)
            vc = v - mu
            var = jnp.mean(vc * vc, axis=-1, keepdims=True)
            vn = (vc * lax.rsqrt(var + EPS) * lng_ref[:, cols] + lnb_ref[:, cols]).astype(BF16)
            sg = jnp.dot(wsm_scr[g], vn, preferred_element_type=F32) + bs_ref[g]
            ya = u_ref[rows, cols].astype(F32) * sg
            ssq = ssq + jnp.sum(ya * ya, axis=-1, keepdims=True)
            acc_scr[rows, cols] = ya
        inv = lax.rsqrt(ssq * (1.0 / D_A) + EPS)
        ya_ref[rows, :] = (acc_scr[rows, :] * inv * ng_ref[...]).astype(BF16)
        return carry

    lax.fori_loop(0, TM_GMLP // CHUNK, chunk_body, 0)


def _gmlp(proj, ln_g, ln_b, w_s, bs_full, norm_g):
    return pl.pallas_call(
        _gmlp_kernel,
        grid=(SEQ // TM_GMLP,),
        in_specs=[
            pl.BlockSpec((TM_GMLP, D_A), lambda i: (i, 0)),
            pl.BlockSpec((TM_GMLP, D_A), lambda i: (i, 1)),
            pl.BlockSpec((1, D_A), lambda i: (0, 0)),
            pl.BlockSpec((1, D_A), lambda i: (0, 0)),
            pl.BlockSpec((N_GROUPS_A, CHUNK, CHUNK), lambda i: (0, 0, 0)),
            pl.BlockSpec((N_GROUPS_A, CHUNK, D_HEAD_A), lambda i: (0, 0, 0)),
            pl.BlockSpec((1, D_A), lambda i: (0, 0)),
        ],
        out_specs=pl.BlockSpec((TM_GMLP, D_A), lambda i: (i, 0)),
        out_shape=jax.ShapeDtypeStruct((SEQ, D_A), BF16),
        scratch_shapes=[
            pltpu.VMEM((N_GROUPS_A, CHUNK, CHUNK), BF16),
            pltpu.VMEM((TM_GMLP, D_A), F32),
        ],
        compiler_params=_params("arbitrary"),
        name="gmlp",
    )(proj, proj, ln_g, ln_b, w_s, bs_full, norm_g)


def _split2(w):
    hi = w.astype(BF16)
    lo = (w - hi.astype(F32)).astype(BF16)
    return hi, lo


def _ssd_kernel(z_ref, xs_ref, bc_ref, dt_ref, cw_ref, cb_ref, dtb_ref, alog_ref, dskip_ref,
                ng_ref, e_ref, yb_ref, xf_scr, xc_scr, h_scr):
    @pl.when(pl.program_id(0) == 0)
    def _():
        xf_scr[0:HALO, :] = jnp.zeros((HALO, D_XBC), F32)
        h_scr[...] = jnp.zeros((D_STATE, D_SSM), F32)

    xf_scr[HALO:HALO + R_SSD, 0:D_SSM] = xs_ref[...].astype(F32)
    xf_scr[HALO:HALO + R_SSD, D_SSM:D_XBC] = bc_ref[...].astype(F32)
    for c0 in range(0, D_XBC, 4 * LANES):
        cols = slice(c0, c0 + 4 * LANES)
        acc = cb_ref[:, cols] + cw_ref[0:1, cols] * xf_scr[HALO - 3:HALO - 3 + R_SSD, cols]
        for k in range(1, SSM_CONV):
            off = HALO - (SSM_CONV - 1) + k
            acc = acc + cw_ref[k:k + 1, cols] * xf_scr[off:off + R_SSD, cols]
        xc_scr[:, cols] = acc * _sigmoid(acc)
    xf_scr[0:HALO, :] = xf_scr[R_SSD:R_SSD + HALO, :]

    row = lax.broadcasted_iota(jnp.int32, (CHUNK, CHUNK), 0)
    lane = lax.broadcasted_iota(jnp.int32, (CHUNK, CHUNK), 1)
    tril = row >= lane
    tri_b = jnp.where(tril, 1.0, 0.0).astype(BF16)
    a_neg = -jnp.exp(alog_ref[...])

    def expand(w):
        hi, lo = _split2(w)
        return (jnp.dot(hi, e_ref[...], preferred_element_type=F32)
                + jnp.dot(lo, e_ref[...], preferred_element_type=F32))

    def chunk_body(c, carry):
        r0 = pl.multiple_of(c * CHUNK, CHUNK)
        rows = pl.ds(r0, CHUNK)
        dtv = dt_ref[rows, :] + dtb_ref[...]
        dt = jnp.maximum(dtv, 0.0) + jnp.log1p(jnp.exp(-jnp.abs(dtv)))
        adt = dt * a_neg
        hi = adt.astype(BF16)
        r1 = adt - hi.astype(F32)
        mid = r1.astype(BF16)
        lo = (r1 - mid.astype(F32)).astype(BF16)
        cs = (jnp.dot(tri_b, hi, preferred_element_type=F32)
              + jnp.dot(tri_b, mid, preferred_element_type=F32)
              + jnp.dot(tri_b, lo, preferred_element_type=F32))
        cs_t = cs.T
        cs_last = cs[CHUNK - 1:CHUNK, :]
        w1e = expand(dt)
        w2e = expand(dt * jnp.exp(cs_last - cs))
        w3e = expand(jnp.exp(cs))

        for g in range(N_SSM_GROUPS):
            gx = slice(g * 2 * LANES, (g + 1) * 2 * LANES)
            b_g = xc_scr[rows, D_SSM + g * D_STATE:D_SSM + (g + 1) * D_STATE]
            c_g = xc_scr[rows, D_SSM + N_SSM_GROUPS * D_STATE + g * D_STATE:
                         D_SSM + N_SSM_GROUPS * D_STATE + (g + 1) * D_STATE]
            cb = c_g.astype(BF16)
            bt = b_g.T.astype(BF16)
            scores = jnp.dot(cb, bt, preferred_element_type=F32)
            xg = xc_scr[rows, gx]
            xdt = xg * w1e[:, gx]
            xw2 = (xg * w2e[:, gx]).astype(BF16)
            hg = h_scr[:, gx]
            y_off = jnp.dot(cb, hg.astype(BF16), preferred_element_type=F32) * w3e[:, gx]
            states = jnp.dot(bt, xw2, preferred_element_type=F32)
            h_scr[:, gx] = w3e[CHUNK - 1:CHUNK, gx] * hg + states

            ys = []
            ssq = jnp.zeros((CHUNK, 1), F32)
            for jp in range(2):
                px = slice(jp * LANES, (jp + 1) * LANES)
                pcols = slice(g * 2 * LANES + jp * LANES, g * 2 * LANES + (jp + 1) * LANES)
                xp = xdt[:, px]
                halves = (jnp.where(lane < SSM_HEAD_DIM, xp, 0.0).astype(BF16),
                          jnp.where(lane >= SSM_HEAD_DIM, xp, 0.0).astype(BF16))
                yd = jnp.zeros((CHUNK, LANES), F32)
                for hh in range(2):
                    h = g * HEADS_PER_GROUP + 2 * jp + hh
                    colv = jnp.sum(jnp.where(lane == h, cs, 0.0), axis=-1, keepdims=True)
                    rowv = cs_t[h:h + 1, :]
                    lmat = jnp.exp(jnp.where(tril, colv - rowv, -jnp.inf))
                    m = (scores * lmat).astype(BF16)
                    yd = yd + jnp.dot(m, halves[hh], preferred_element_type=F32)
                y = yd + y_off[:, px] + dskip_ref[:, pcols] * xg[:, px]
                zz = z_ref[rows, pcols].astype(F32)
                y = y * (zz * _sigmoid(zz))
                ssq = ssq + jnp.sum(y * y, axis=-1, keepdims=True)
                ys.append((pcols, y))
            inv = lax.rsqrt(ssq * (1.0 / (2 * LANES)) + EPS)
            for pcols, y in ys:
                yb_ref[rows, pcols] = (y * inv * ng_ref[:, pcols]).astype(BF16)
        return carry

    lax.fori_loop(0, R_SSD // CHUNK, chunk_body, 0)


def _ssd(proj, dt_raw, conv_w, conv_b, dt_bias, a_log, d_skip_e, norm_g, expand_m):
    blk = lambda col: pl.BlockSpec((R_SSD, D_SSM), lambda i: (i, col))
    full = lambda shape: pl.BlockSpec(shape, lambda i: (0,) * len(shape))
    return pl.pallas_call(
        _ssd_kernel,
        grid=(SEQ // R_SSD,),
        in_specs=[
            blk(2), blk(3), blk(4),
            pl.BlockSpec((R_SSD, LANES), lambda i: (i, 0)),
            full((SSM_CONV, D_XBC)), full((1, D_XBC)), full((1, LANES)), full((1, LANES)),
            full((1, D_SSM)), full((1, D_SSM)), full((LANES, D_SSM)),
        ],
        out_specs=pl.BlockSpec((R_SSD, D_SSM), lambda i: (i, 0)),
        out_shape=jax.ShapeDtypeStruct((SEQ, D_SSM), BF16),
        scratch_shapes=[
            pltpu.VMEM((R_SSD + HALO, D_XBC), F32),
            pltpu.VMEM((R_SSD, D_XBC), F32),
            pltpu.VMEM((D_STATE, D_SSM), F32),
        ],
        compiler_params=_params("arbitrary"),
        name="ssd",
    )(proj, proj, proj, dt_raw, conv_w, conv_b, dt_bias, a_log, d_skip_e, norm_g, expand_m)


def _proj_res_norm_kernel(n_lhs, tn, *refs):
    lhs_refs = refs[:n_lhs]
    w_ref, res_ref, g_ref, h_ref, n_ref = refs[n_lhs:]
    j = pl.program_id(1)
    k0 = 0
    acc = res_ref[...]
    for lhs in lhs_refs:
        k1 = k0 + lhs.shape[1]
        acc = acc + jnp.dot(lhs[...], w_ref[k0:k1, :], preferred_element_type=F32)
        k0 = k1
    h_ref[:, pl.ds(pl.multiple_of(j * tn, tn), tn)] = acc

    @pl.when(j == pl.num_programs(1) - 1)
    def _():
        h = h_ref[...]
        n_ref[...] = (h * _rms_scale(h) * g_ref[...]).astype(BF16)


def _proj_res_norm(lhs_list, w, res, g, tm, tn, name):
    k_total = sum(l.shape[1] for l in lhs_list)
    n_out = w.shape[1]
    in_specs = [pl.BlockSpec((tm, l.shape[1]), lambda i, j: (i, 0)) for l in lhs_list]
    in_specs += [
        pl.BlockSpec((k_total, tn), lambda i, j: (0, j)),
        pl.BlockSpec((tm, tn), lambda i, j: (i, j)),
        pl.BlockSpec((1, n_out), lambda i, j: (0, 0)),
    ]
    return pl.pallas_call(
        functools.partial(_proj_res_norm_kernel, len(lhs_list), tn),
        grid=(SEQ // tm, n_out // tn),
        in_specs=in_specs,
        out_specs=[
            pl.BlockSpec((tm, n_out), lambda i, j: (i, 0)),
            pl.BlockSpec((tm, n_out), lambda i, j: (i, 0)),
        ],
        out_shape=[
            jax.ShapeDtypeStruct((SEQ, n_out), F32),
            jax.ShapeDtypeStruct((SEQ, n_out), BF16),
        ],
        compiler_params=_params("arbitrary", "arbitrary"),
        name=name,
    )(*lhs_list, w, res, g)


def _up_kernel(f_ref, wg_ref, wu_ref, cwg_ref, cwu_ref, cbg_ref, cbu_ref, act_ref, st_scr, halo_scr):
    i = pl.program_id(0)
    j = pl.program_id(1)

    @pl.when(i == 0)
    def _():
        halo_scr[j] = jnp.zeros((HALO, 2 * TN_UP), F32)

    f = f_ref[...]
    st_scr[0:HALO, :] = halo_scr[j]
    st_scr[HALO:HALO + TM_UP, 0:TN_UP] = jnp.dot(f, wg_ref[...], preferred_element_type=F32)
    st_scr[HALO:HALO + TM_UP, TN_UP:2 * TN_UP] = jnp.dot(f, wu_ref[...], preferred_element_type=F32)
    halo_scr[j] = st_scr[TM_UP:TM_UP + HALO, :]

    def conv(cols, cw_ref, cb_ref):
        acc = cb_ref[...]
        for k in range(FFN_CONV):
            off = HALO - (FFN_CONV - 1) + k
            acc = acc + cw_ref[k:k + 1, :] * st_scr[off:off + TM_UP, cols]
        return acc

    gate = conv(slice(0, TN_UP), cwg_ref, cbg_ref)
    up = conv(slice(TN_UP, 2 * TN_UP), cwu_ref, cbu_ref)
    act_ref[...] = (gate * _sigmoid(gate) * up).astype(BF16)


def _up_proj(f, w_up, conv_w, conv_b):
    n_t = D_FF // TN_UP
    return pl.pallas_call(
        _up_kernel,
        grid=(SEQ // TM_UP, n_t),
        in_specs=[
            pl.BlockSpec((TM_UP, D_MODEL), lambda i, j: (i, 0)),
            pl.BlockSpec((D_MODEL, TN_UP), lambda i, j: (0, j)),
            pl.BlockSpec((D_MODEL, TN_UP), lambda i, j: (0, j + n_t)),
            pl.BlockSpec((FFN_CONV, TN_UP), lambda i, j: (0, j)),
            pl.BlockSpec((FFN_CONV, TN_UP), lambda i, j: (0, j + n_t)),
            pl.BlockSpec((1, TN_UP), lambda i, j: (0, j)),
            pl.BlockSpec((1, TN_UP), lambda i, j: (0, j + n_t)),
        ],
        out_specs=pl.BlockSpec((TM_UP, TN_UP), lambda i, j: (i, j)),
        out_shape=jax.ShapeDtypeStruct((SEQ, D_FF), BF16),
        scratch_shapes=[
            pltpu.VMEM((TM_UP + HALO, 2 * TN_UP), F32),
            pltpu.VMEM((n_t, HALO, 2 * TN_UP), F32),
        ],
        compiler_params=_params("arbitrary", "arbitrary"),
        name="up_proj",
    )(f, w_up, w_up, conv_w, conv_w, conv_b, conv_b)


def _ple_kernel(n_ref, wpg_ref, p_ref, wple_ref, h_ref, g_ref, o_ref):
    gate = _sigmoid(jnp.dot(n_ref[...], wpg_ref[...], preferred_element_type=F32))
    emb = jnp.dot(p_ref[...].astype(BF16), wple_ref[...], preferred_element_type=F32)
    h = h_ref[...] + gate * emb
    o_ref[...] = h * _rms_scale(h) * g_ref[...]


def _ple(n3, w_pg, p, w_ple, h2, g):
    return pl.pallas_call(
        _ple_kernel,
        grid=(SEQ // TM_PLE,),
        in_specs=[
            pl.BlockSpec((TM_PLE, D_MODEL), lambda i: (i, 0)),
            pl.BlockSpec((D_MODEL, D_MODEL), lambda i: (0, 0)),
            pl.BlockSpec((TM_PLE, D_PLE), lambda i: (i, 0)),
            pl.BlockSpec((D_PLE, D_MODEL), lambda i: (0, 0)),
            pl.BlockSpec((TM_PLE, D_MODEL), lambda i: (i, 0)),
            pl.BlockSpec((1, D_MODEL), lambda i: (0, 0)),
        ],
        out_specs=pl.BlockSpec((TM_PLE, D_MODEL), lambda i: (i, 0)),
        out_shape=jax.ShapeDtypeStruct((SEQ, D_MODEL), F32),
        compiler_params=_params("arbitrary"),
        name="ple",
    )(n3, w_pg, p, w_ple, h2, g)


def _layer(h, p_i, norm_mix_g, w_in, ln_a_g, ln_a_b, w_s, b_s, norm_a_g, conv_ssm_w, conv_ssm_b,
           dt_bias, a_log, d_skip, ssm_norm_g, w_out, norm_ffn_g, w_up, conv_ffn_w, conv_ffn_b,
           w_down, norm_ple_g, w_ple_gate, w_ple):
    row = lambda v: v.reshape(1, -1)
    pad_lanes = lambda v: jnp.pad(v, (0, LANES - v.shape[0])).reshape(1, LANES)

    w_main = w_in[:, :D_PROJ].astype(BF16)
    w_dt = jnp.pad(w_in[:, D_PROJ:], ((0, 0), (0, LANES - N_SSM_HEADS))).astype(BF16)
    proj, dt_raw = _in_proj(h, row(norm_mix_g), w_main, w_dt)

    bs_full = jnp.broadcast_to(b_s[:, :, None], (N_GROUPS_A, CHUNK, D_HEAD_A))
    y_a = _gmlp(proj, row(ln_a_g), row(ln_a_b), w_s, bs_full, row(norm_a_g))

    head_of_lane = jnp.arange(D_SSM, dtype=jnp.int32) // SSM_HEAD_DIM
    expand_m = (jnp.arange(LANES, dtype=jnp.int32)[:, None] == head_of_lane[None, :]).astype(BF16)
    y_b = _ssd(proj, dt_raw, conv_ssm_w, row(conv_ssm_b), pad_lanes(dt_bias), pad_lanes(a_log),
               row(jnp.repeat(d_skip, SSM_HEAD_DIM)), row(ssm_norm_g), expand_m)

    h1, f = _proj_res_norm([y_a, y_b], w_out.astype(BF16), h, row(norm_ffn_g),
                           TM_OUT, TN_OUT, "out_proj")
    act = _up_proj(f, w_up.astype(BF16), conv_ffn_w, row(conv_ffn_b))
    h2, n3 = _proj_res_norm([act], w_down.astype(BF16), h1, row(norm_ple_g),
                            TM_DOWN, TN_DOWN, "down_proj")
    return h2, n3, w_ple_gate.astype(BF16), p_i, w_ple.astype(BF16)


def kernel(x, p, norm_mix_g, w_in, ln_a_g, ln_a_b, w_s, b_s, norm_a_g, conv_ssm_w, conv_ssm_b,
           dt_bias, a_log, d_skip, ssm_norm_g, w_out, norm_ffn_g, w_up, conv_ffn_w, conv_ffn_b,
           w_down, norm_ple_g, w_ple_gate, w_ple, norm_final_g):
    assert x.shape == (1, SEQ, D_MODEL) and w_in.shape[0] == 1
    h = x[0]
    h2, n3, w_pg, p_i, w_pl = _layer(
        h, p[0, 0], norm_mix_g[0], w_in[0], ln_a_g[0], ln_a_b[0], w_s[0], b_s[0], norm_a_g[0],
        conv_ssm_w[0], conv_ssm_b[0], dt_bias[0], a_log[0], d_skip[0], ssm_norm_g[0], w_out[0],
        norm_ffn_g[0], w_up[0], conv_ffn_w[0], conv_ffn_b[0], w_down[0], norm_ple_g[0],
        w_ple_gate[0], w_ple[0])
    out = _ple(n3, w_pg, p_i, w_pl, h2, norm_final_g.reshape(1, -1))
    return out[None]
```

```python
import functools
import math

import jax
import jax.numpy as jnp
from jax import lax
from jax.experimental import pallas as pl
from jax.experimental.pallas import tpu as pltpu

F32 = jnp.float32
BF16 = jnp.bfloat16

D_MODEL = 2048
SEQ = 8192
D_A = 2048
CHUNK = 128
N_GROUPS_A = 16
D_HEAD_A = 128
D_SSM = 2048
SSM_HEAD_DIM = 64
N_SSM_HEADS = 32
N_SSM_GROUPS = 8
HEADS_PER_GROUP = 4
D_STATE = 128
SSM_CONV = 4
D_XBC = 4096
D_PROJ = 10240
D_FF = 5632
FFN_CONV = 3
D_PLE = 256
EPS = 1e-6
LOG2E = 1.0 / math.log(2.0)

LANES = 128
SUBLANES = 8
HALO = SUBLANES
VMEM_LIMIT_BYTES = 56 * 1024 * 1024

TM_IN = 1024
TN_IN = 512
N_UV_TILES = (2 * D_A) // TN_IN
N_Z_END = (2 * D_A + D_SSM) // TN_IN
TM_GMLP = 512
R_SSD = 512
TM_OUT = 512
TN_OUT = 512
TM_UP = 1024
TN_UP = 512
TM_DOWN = 512
TN_DOWN = 512
TM_PLE = 512


def _params(*sem):
    return pltpu.CompilerParams(dimension_semantics=sem, vmem_limit_bytes=VMEM_LIMIT_BYTES)


def _sigmoid(x):
    return 1.0 / (1.0 + jnp.exp(-x))


def _rms_scale(x):
    return lax.rsqrt(jnp.mean(x * x, axis=-1, keepdims=True) + EPS)


def _in_proj_kernel(x_ref, g_ref, w_ref, wdt_ref, cw_ref, cb_ref, proj_ref, dt_ref,
                    a_scr, st_scr, halo_scr):
    i = pl.program_id(0)
    j = pl.program_id(1)
    t = jnp.maximum(j - N_Z_END, 0)

    @pl.when(j == 0)
    def _():
        x = x_ref[...]
        a = (x * _rms_scale(x) * g_ref[...]).astype(BF16)
        a_scr[...] = a
        dt_ref[...] = jnp.dot(a, wdt_ref[...], preferred_element_type=F32)

    @pl.when((i == 0) & (j >= N_Z_END))
    def _():
        halo_scr[t] = jnp.zeros((HALO, TN_IN), F32)

    @pl.when(j < N_UV_TILES)
    def _():
        y = jnp.dot(a_scr[...], w_ref[...], preferred_element_type=F32)
        proj_ref[...] = jax.nn.gelu(y).astype(BF16)

    @pl.when((j >= N_UV_TILES) & (j < N_Z_END))
    def _():
        y = jnp.dot(a_scr[...], w_ref[...], preferred_element_type=F32)
        proj_ref[...] = (y * _sigmoid(y)).astype(BF16)

    @pl.when(j >= N_Z_END)
    def _():
        st_scr[0:HALO, :] = halo_scr[t]
        st_scr[HALO:HALO + TM_IN, :] = jnp.dot(a_scr[...], w_ref[...], preferred_element_type=F32)
        halo_scr[t] = st_scr[TM_IN:TM_IN + HALO, :]
        acc = cb_ref[...]
        for k in range(SSM_CONV):
            off = HALO - (SSM_CONV - 1) + k
            acc = acc + cw_ref[k:k + 1, :] * st_scr[off:off + TM_IN, :]
        proj_ref[...] = (acc * _sigmoid(acc)).astype(BF16)


def _in_proj(x, g, w, wdt, conv_w, conv_b):
    xbc_tile = lambda i, j: (0, jnp.maximum(j - N_Z_END, 0))
    return pl.pallas_call(
        _in_proj_kernel,
        grid=(SEQ // TM_IN, D_PROJ // TN_IN),
        in_specs=[
            pl.BlockSpec((TM_IN, D_MODEL), lambda i, j: (i, 0)),
            pl.BlockSpec((1, D_MODEL), lambda i, j: (0, 0)),
            pl.BlockSpec((D_MODEL, TN_IN), lambda i, j: (0, j)),
            pl.BlockSpec((D_MODEL, LANES), lambda i, j: (0, 0)),
            pl.BlockSpec((SSM_CONV, TN_IN), xbc_tile),
            pl.BlockSpec((1, TN_IN), xbc_tile),
        ],
        out_specs=[
            pl.BlockSpec((TM_IN, TN_IN), lambda i, j: (i, j)),
            pl.BlockSpec((TM_IN, LANES), lambda i, j: (i, 0)),
        ],
        out_shape=[
            jax.ShapeDtypeStruct((SEQ, D_PROJ), BF16),
            jax.ShapeDtypeStruct((SEQ, LANES), F32),
        ],
        scratch_shapes=[
            pltpu.VMEM((TM_IN, D_MODEL), BF16),
            pltpu.VMEM((TM_IN + HALO, TN_IN), F32),
            pltpu.VMEM((D_XBC // TN_IN, HALO, TN_IN), F32),
        ],
        compiler_params=_params("arbitrary", "arbitrary"),
        name="in_proj",
    )(x, g, w, wdt, conv_w, conv_b)


def _gmlp_kernel(u_ref, v_ref, lng_ref, lnb_ref, ws_ref, bs_ref, ng_ref, ya_ref, wsm_scr, acc_scr):
    @pl.when(pl.program_id(0) == 0)
    def _():
        row = lax.broadcasted_iota(jnp.int32, (CHUNK, CHUNK), 0)
        col = lax.broadcasted_iota(jnp.int32, (CHUNK, CHUNK), 1)
        for g in range(N_GROUPS_A):
            wsm_scr[g] = jnp.where(row >= col, ws_ref[g], 0.0).astype(BF16)

    def chunk_body(c, carry):
        r0 = pl.multiple_of(c * CHUNK, CHUNK)
        rows = pl.ds(r0, CHUNK)
        ssq = jnp.zeros((CHUNK, 1), F32)
        for g in range(N_GROUPS_A):
            cols = slice(g * D_HEAD_A, (g + 1) * D_HEAD_A)
            v = v_ref[rows, cols].astype(F32)
            mu = jnp.mean(v, axis=-1, keepdims=True)
            vc = v - mu
            var = jnp.mean(vc * vc, axis=-1, keepdims=True)
            vn = (vc * lax.rsqrt(var + EPS) * lng_ref[:, cols] + lnb_ref[:, cols]).astype(BF16)
            sg = jnp.dot(wsm_scr[g], vn, preferred_element_type=F32) + bs_ref[g]
            ya = u_ref[rows, cols].astype(F32) * sg
            ssq = ssq + jnp.sum(ya * ya, axis=-1, keepdims=True)
            acc_scr[rows, cols] = ya
        inv = lax.rsqrt(ssq * (1.0 / D_A) + EPS)
        ya_ref[rows, :] = (acc_scr[rows, :] * inv * ng_ref[...]).astype(BF16)
        return carry

    lax.fori_loop(0, TM_GMLP // CHUNK, chunk_body, 0)


def _gmlp(proj, ln_g, ln_b, w_s, bs_full, norm_g):
    return pl.pallas_call(
        _gmlp_kernel,
        grid=(SEQ // TM_GMLP,),
        in_specs=[
            pl.BlockSpec((TM_GMLP, D_A), lambda i: (i, 0)),
            pl.BlockSpec((TM_GMLP, D_A), lambda i: (i, 1)),
            pl.BlockSpec((1, D_A), lambda i: (0, 0)),
            pl.BlockSpec((1, D_A), lambda i: (0, 0)),
            pl.BlockSpec((N_GROUPS_A, CHUNK, CHUNK), lambda i: (0, 0, 0)),
            pl.BlockSpec((N_GROUPS_A, CHUNK, D_HEAD_A), lambda i: (0, 0, 0)),
            pl.BlockSpec((1, D_A), lambda i: (0, 0)),
        ],
        out_specs=pl.BlockSpec((TM_GMLP, D_A), lambda i: (i, 0)),
        out_shape=jax.ShapeDtypeStruct((SEQ, D_A), BF16),
        scratch_shapes=[
            pltpu.VMEM((N_GROUPS_A, CHUNK, CHUNK), BF16),
            pltpu.VMEM((TM_GMLP, D_A), F32),
        ],
        compiler_params=_params("arbitrary"),
        name="gmlp",
    )(proj, proj, ln_g, ln_b, w_s, bs_full, norm_g)


def _ssd_kernel(zs_ref, xs_ref, bc_ref, dt_ref, dtb_ref, alog_ref, dskip_ref, ng_ref, yb_ref,
                h_scr, cs2_scr, srct_scr, w2t_scr):
    @pl.when(pl.program_id(0) == 0)
    def _():
        h_scr[...] = jnp.zeros((D_STATE, D_SSM), F32)

    row = lax.broadcasted_iota(jnp.int32, (CHUNK, CHUNK), 0)
    lane = lax.broadcasted_iota(jnp.int32, (CHUNK, CHUNK), 1)
    tril = row >= lane
    tri_b = jnp.where(tril, 1.0, 0.0).astype(BF16)
    left = lane < SSM_HEAD_DIM
    mask_l = jnp.where(left, 1.0, 0.0).astype(BF16)
    mask_r = jnp.where(left, 0.0, 1.0).astype(BF16)
    a_neg = -jnp.exp(alog_ref[...])

    dtv = dt_ref[...] + dtb_ref[...]
    dt_all = jnp.maximum(dtv, 0.0) + jnp.log1p(jnp.exp(-jnp.abs(dtv)))
    for c in range(R_SSD // CHUNK):
        crows = slice(c * CHUNK, (c + 1) * CHUNK)
        dt = dt_all[crows, :]
        adt = dt * a_neg
        hi = adt.astype(BF16)
        r1 = adt - hi.astype(F32)
        mid = r1.astype(BF16)
        lo = (r1 - mid.astype(F32)).astype(BF16)
        cs = (jnp.dot(tri_b, hi, preferred_element_type=F32)
              + jnp.dot(tri_b, mid, preferred_element_type=F32)
              + jnp.dot(tri_b, lo, preferred_element_type=F32))
        cs2 = cs * LOG2E
        cs2_scr[crows, :] = cs2
        srct_scr[crows, :] = (cs2 - jnp.log2(dt)).T
        w2t_scr[crows, :] = (dt * jnp.exp2(cs2[CHUNK - 1:CHUNK, :] - cs2)).T

    def chunk_body(c, carry):
        r0 = pl.multiple_of(c * CHUNK, CHUNK)
        rows = pl.ds(r0, CHUNK)
        cs2 = cs2_scr[rows, :]
        src_t = srct_scr[rows, :]
        w2_t = w2t_scr[rows, :]

        for g in range(N_SSM_GROUPS):
            b_g = bc_ref[rows, g * D_STATE:(g + 1) * D_STATE]
            c_g = bc_ref[rows, N_SSM_GROUPS * D_STATE + g * D_STATE:
                         N_SSM_GROUPS * D_STATE + (g + 1) * D_STATE]
            c_f = c_g.astype(F32)
            bt_f = b_g.astype(F32).T
            scores = jnp.dot(c_g, bt_f.astype(BF16), preferred_element_type=F32)

            ys = []
            ssq = jnp.zeros((CHUNK, 1), F32)
            for jp in range(2):
                pcols = slice((2 * g + jp) * LANES, (2 * g + jp + 1) * LANES)
                xp = xs_ref[rows, pcols]
                hp = h_scr[:, pcols]
                hpb = hp.astype(BF16)
                lhs_y, lhs_s, decay = [], [], []
                for hh in range(2):
                    h = g * HEADS_PER_GROUP + 2 * jp + hh
                    colv = jnp.sum(jnp.where(lane == h, cs2, 0.0), axis=-1, keepdims=True)
                    lmat = jnp.exp2(jnp.where(tril, colv - src_t[h:h + 1, :], -jnp.inf))
                    ecol = jnp.exp2(colv)
                    lhs_y += [(scores * lmat).astype(BF16), (c_f * ecol).astype(BF16)]
                    lhs_s.append((bt_f * w2_t[h:h + 1, :]).astype(BF16))
                    decay.append(jnp.exp2(colv[CHUNK - 1:CHUNK, :]))
                x_l, x_r = xp * mask_l, xp * mask_r
                yd = jnp.dot(jnp.concatenate(lhs_y, axis=1),
                             jnp.concatenate([x_l, hpb * mask_l, x_r, hpb * mask_r], axis=0),
                             preferred_element_type=F32)
                states = jnp.dot(jnp.concatenate(lhs_s, axis=1),
                                 jnp.concatenate([x_l, x_r], axis=0),
                                 preferred_element_type=F32)
                h_scr[:, pcols] = jnp.where(left, decay[0], decay[1]) * hp + states

                y = yd + dskip_ref[:, pcols] * xp.astype(F32)
                y = y * zs_ref[rows, pcols].astype(F32)
                ssq = ssq + jnp.sum(y * y, axis=-1, keepdims=True)
                ys.append((pcols, y))
            inv = lax.rsqrt(ssq * (1.0 / (2 * LANES)) + EPS)
            for pcols, y in ys:
                yb_ref[rows, pcols] = (y * inv * ng_ref[:, pcols]).astype(BF16)
        return carry

    lax.fori_loop(0, R_SSD // CHUNK, chunk_body, 0)


def _ssd(proj, dt_raw, dt_bias, a_log, d_skip_e, norm_g):
    blk = lambda col: pl.BlockSpec((R_SSD, D_SSM), lambda i: (i, col))
    full = lambda shape: pl.BlockSpec(shape, lambda i: (0,) * len(shape))
    return pl.pallas_call(
        _ssd_kernel,
        grid=(SEQ // R_SSD,),
        in_specs=[
            blk(2), blk(3), blk(4),
            pl.BlockSpec((R_SSD, LANES), lambda i: (i, 0)),
            full((1, LANES)), full((1, LANES)), full((1, D_SSM)), full((1, D_SSM)),
        ],
        out_specs=pl.BlockSpec((R_SSD, D_SSM), lambda i: (i, 0)),
        out_shape=jax.ShapeDtypeStruct((SEQ, D_SSM), BF16),
        scratch_shapes=[pltpu.VMEM((D_STATE, D_SSM), F32)] + [pltpu.VMEM((R_SSD, LANES), F32)] * 3,
        compiler_params=_params("arbitrary"),
        name="ssd",
    )(proj, proj, proj, dt_raw, dt_bias, a_log, d_skip_e, norm_g)


def _proj_res_norm_kernel(n_lhs, tn, *refs):
    lhs_refs = refs[:n_lhs]
    w_ref, res_ref, g_ref, h_ref, n_ref = refs[n_lhs:]
    j = pl.program_id(1)
    k0 = 0
    acc = res_ref[...]
    for lhs in lhs_refs:
        k1 = k0 + lhs.shape[1]
        acc = acc + jnp.dot(lhs[...], w_ref[k0:k1, :], preferred_element_type=F32)
        k0 = k1
    h_ref[:, pl.ds(pl.multiple_of(j * tn, tn), tn)] = acc

    @pl.when(j == pl.num_programs(1) - 1)
    def _():
        h = h_ref[...]
        n_ref[...] = (h * _rms_scale(h) * g_ref[...]).astype(BF16)


def _proj_res_norm(lhs_list, w, res, g, tm, tn, name):
    k_total = sum(l.shape[1] for l in lhs_list)
    n_out = w.shape[1]
    in_specs = [pl.BlockSpec((tm, l.shape[1]), lambda i, j: (i, 0)) for l in lhs_list]
    in_specs += [
        pl.BlockSpec((k_total, tn), lambda i, j: (0, j)),
        pl.BlockSpec((tm, tn), lambda i, j: (i, j)),
        pl.BlockSpec((1, n_out), lambda i, j: (0, 0)),
    ]
    return pl.pallas_call(
        functools.partial(_proj_res_norm_kernel, len(lhs_list), tn),
        grid=(SEQ // tm, n_out // tn),
        in_specs=in_specs,
        out_specs=[
            pl.BlockSpec((tm, n_out), lambda i, j: (i, 0)),
            pl.BlockSpec((tm, n_out), lambda i, j: (i, 0)),
        ],
        out_shape=[
            jax.ShapeDtypeStruct((SEQ, n_out), F32),
            jax.ShapeDtypeStruct((SEQ, n_out), BF16),
        ],
        compiler_params=_params("arbitrary", "arbitrary"),
        name=name,
    )(*lhs_list, w, res, g)


def _up_kernel(f_ref, wg_ref, wu_ref, cwg_ref, cwu_ref, cbg_ref, cbu_ref, act_ref, st_scr, halo_scr):
    i = pl.program_id(0)
    j = pl.program_id(1)

    @pl.when(i == 0)
    def _():
        halo_scr[j] = jnp.zeros((HALO, 2 * TN_UP), F32)

    f = f_ref[...]
    st_scr[0:HALO, :] = halo_scr[j]
    st_scr[HALO:HALO + TM_UP, 0:TN_UP] = jnp.dot(f, wg_ref[...], preferred_element_type=F32)
    st_scr[HALO:HALO + TM_UP, TN_UP:2 * TN_UP] = jnp.dot(f, wu_ref[...], preferred_element_type=F32)
    halo_scr[j] = st_scr[TM_UP:TM_UP + HALO, :]

    def conv(cols, cw_ref, cb_ref):
        acc = cb_ref[...]
        for k in range(FFN_CONV):
            off = HALO - (FFN_CONV - 1) + k
            acc = acc + cw_ref[k:k + 1, :] * st_scr[off:off + TM_UP, cols]
        return acc

    gate = conv(slice(0, TN_UP), cwg_ref, cbg_ref)
    up = conv(slice(TN_UP, 2 * TN_UP), cwu_ref, cbu_ref)
    act_ref[...] = (gate * _sigmoid(gate) * up).astype(BF16)


def _up_proj(f, w_up, conv_w, conv_b):
    n_t = D_FF // TN_UP
    return pl.pallas_call(
        _up_kernel,
        grid=(SEQ // TM_UP, n_t),
        in_specs=[
            pl.BlockSpec((TM_UP, D_MODEL), lambda i, j: (i, 0)),
            pl.BlockSpec((D_MODEL, TN_UP), lambda i, j: (0, j)),
            pl.BlockSpec((D_MODEL, TN_UP), lambda i, j: (0, j + n_t)),
            pl.BlockSpec((FFN_CONV, TN_UP), lambda i, j: (0, j)),
            pl.BlockSpec((FFN_CONV, TN_UP), lambda i, j: (0, j + n_t)),
            pl.BlockSpec((1, TN_UP), lambda i, j: (0, j)),
            pl.BlockSpec((1, TN_UP), lambda i, j: (0, j + n_t)),
        ],
        out_specs=pl.BlockSpec((TM_UP, TN_UP), lambda i, j: (i, j)),
        out_shape=jax.ShapeDtypeStruct((SEQ, D_FF), BF16),
        scratch_shapes=[
            pltpu.VMEM((TM_UP + HALO, 2 * TN_UP), F32),
            pltpu.VMEM((n_t, HALO, 2 * TN_UP), F32),
        ],
        compiler_params=_params("arbitrary", "arbitrary"),
        name="up_proj",
    )(f, w_up, w_up, conv_w, conv_w, conv_b, conv_b)


def _ple_kernel(n_ref, wpg_ref, p_ref, wple_ref, h_ref, g_ref, o_ref):
    gate = _sigmoid(jnp.dot(n_ref[...], wpg_ref[...], preferred_element_type=F32))
    emb = jnp.dot(p_ref[...].astype(BF16), wple_ref[...], preferred_element_type=F32)
    h = h_ref[...] + gate * emb
    o_ref[...] = h * _rms_scale(h) * g_ref[...]


def _ple(n3, w_pg, p, w_ple, h2, g):
    return pl.pallas_call(
        _ple_kernel,
        grid=(SEQ // TM_PLE,),
        in_specs=[
            pl.BlockSpec((TM_PLE, D_MODEL), lambda i: (i, 0)),
            pl.BlockSpec((D_MODEL, D_MODEL), lambda i: (0, 0)),
            pl.BlockSpec((TM_PLE, D_PLE), lambda i: (i, 0)),
            pl.BlockSpec((D_PLE, D_MODEL), lambda i: (0, 0)),
            pl.BlockSpec((TM_PLE, D_MODEL), lambda i: (i, 0)),
            pl.BlockSpec((1, D_MODEL), lambda i: (0, 0)),
        ],
        out_specs=pl.BlockSpec((TM_PLE, D_MODEL), lambda i: (i, 0)),
        out_shape=jax.ShapeDtypeStruct((SEQ, D_MODEL), F32),
        compiler_params=_params("arbitrary"),
        name="ple",
    )(n3, w_pg, p, w_ple, h2, g)


def _layer(h, p_i, norm_mix_g, w_in, ln_a_g, ln_a_b, w_s, b_s, norm_a_g, conv_ssm_w, conv_ssm_b,
           dt_bias, a_log, d_skip, ssm_norm_g, w_out, norm_ffn_g, w_up, conv_ffn_w, conv_ffn_b,
           w_down, norm_ple_g, w_ple_gate, w_ple):
    row = lambda v: v.reshape(1, -1)
    pad_lanes = lambda v: jnp.pad(v, (0, LANES - v.shape[0])).reshape(1, LANES)

    w_dt = jnp.pad(w_in[:, D_PROJ:], ((0, 0), (0, LANES - N_SSM_HEADS))).astype(BF16)
    proj, dt_raw = _in_proj(h, row(norm_mix_g), w_in.astype(BF16), w_dt, conv_ssm_w, row(conv_ssm_b))

    bs_full = jnp.broadcast_to(b_s[:, :, None], (N_GROUPS_A, CHUNK, D_HEAD_A))
    y_a = _gmlp(proj, row(ln_a_g), row(ln_a_b), w_s, bs_full, row(norm_a_g))

    y_b = _ssd(proj, dt_raw, pad_lanes(dt_bias), pad_lanes(a_log),
               row(jnp.repeat(d_skip, SSM_HEAD_DIM)), row(ssm_norm_g))

    h1, f = _proj_res_norm([y_a, y_b], w_out.astype(BF16), h, row(norm_ffn_g),
                           TM_OUT, TN_OUT, "out_proj")
    act = _up_proj(f, w_up.astype(BF16), conv_ffn_w, row(conv_ffn_b))
    h2, n3 = _proj_res_norm([act], w_down.astype(BF16), h1, row(norm_ple_g),
                            TM_DOWN, TN_DOWN, "down_proj")
    return h2, n3, w_ple_gate.astype(BF16), p_i, w_ple.astype(BF16)


def kernel(x, p, norm_mix_g, w_in, ln_a_g, ln_a_b, w_s, b_s, norm_a_g, conv_ssm_w, conv_ssm_b,
           dt_bias, a_log, d_skip, ssm_norm_g, w_out, norm_ffn_g, w_up, conv_ffn_w, conv_ffn_b,
           w_down, norm_ple_g, w_ple_gate, w_ple, norm_final_g):
    assert x.shape == (1, SEQ, D_MODEL) and w_in.shape[0] == 1
    h = x[0]
    h2, n3, w_pg, p_i, w_pl = _layer(
        h, p[0, 0], norm_mix_g[0], w_in[0], ln_a_g[0], ln_a_b[0], w_s[0], b_s[0], norm_a_g[0],
        conv_ssm_w[0], conv_ssm_b[0], dt_bias[0], a_log[0], d_skip[0], ssm_norm_g[0], w_out[0],
        norm_ffn_g[0], w_up[0], conv_ffn_w[0], conv_ffn_b[0], w_down[0], norm_ple_g[0],
        w_ple_gate[0], w_ple[0])
    out = _ple(n3, w_pg, p_i, w_pl, h2, norm_final_g.reshape(1, -1))
    return out[None]
```

```python
import functools
import math

import jax
import jax.numpy as jnp
from jax import lax
from jax.experimental import pallas as pl
from jax.experimental.pallas import tpu as pltpu

F32 = jnp.float32
BF16 = jnp.bfloat16

D_MODEL = 2048
SEQ = 8192
D_A = 2048
CHUNK = 128
N_GROUPS_A = 16
D_HEAD_A = 128
D_SSM = 2048
SSM_HEAD_DIM = 64
N_SSM_HEADS = 32
N_SSM_GROUPS = 8
HEADS_PER_GROUP = 4
D_STATE = 128
SSM_CONV = 4
D_XBC = 4096
D_PROJ = 10240
D_FF = 5632
FFN_CONV = 3
D_PLE = 256
EPS = 1e-6
LOG2E = 1.0 / math.log(2.0)

LANES = 128
SUBLANES = 8
HALO = SUBLANES
VMEM_LIMIT_BYTES = 56 * 1024 * 1024

TM_IN = 1024
TN_IN = 512
N_UV_TILES = (2 * D_A) // TN_IN
N_Z_END = (2 * D_A + D_SSM) // TN_IN
SUB_M = 256
TM_GMLP = 512
R_SSD = 512
TM_OUT = 512
TN_OUT = 512
TM_UP = 1024
TN_UP = 512
TM_DOWN = 512
TN_DOWN = 512
TM_PLE = 512


def _params(*sem):
    return pltpu.CompilerParams(dimension_semantics=sem, vmem_limit_bytes=VMEM_LIMIT_BYTES)


def _sigmoid(x):
    return 1.0 / (1.0 + jnp.exp(-x))


def _rms_scale(x):
    return lax.rsqrt(jnp.mean(x * x, axis=-1, keepdims=True) + EPS)


def _in_proj_kernel(x_ref, g_ref, w_ref, wdt_ref, cw_ref, cb_ref, proj_ref, dt_ref,
                    a_scr, st_scr, halo_scr):
    i = pl.program_id(0)
    j = pl.program_id(1)
    t = jnp.maximum(j - N_Z_END, 0)

    @pl.when(j == 0)
    def _():
        x = x_ref[...]
        a = (x * _rms_scale(x) * g_ref[...]).astype(BF16)
        a_scr[...] = a
        dt_ref[...] = jnp.dot(a, wdt_ref[...], preferred_element_type=F32)

    @pl.when((i == 0) & (j >= N_Z_END))
    def _():
        halo_scr[t] = jnp.zeros((HALO, TN_IN), F32)

    def sub_dot(r):
        return jnp.dot(a_scr[r * SUB_M:(r + 1) * SUB_M, :], w_ref[...], preferred_element_type=F32)

    @pl.when(j < N_UV_TILES)
    def _():
        for r in range(TM_IN // SUB_M):
            proj_ref[r * SUB_M:(r + 1) * SUB_M, :] = jax.nn.gelu(sub_dot(r)).astype(BF16)

    @pl.when((j >= N_UV_TILES) & (j < N_Z_END))
    def _():
        for r in range(TM_IN // SUB_M):
            y = sub_dot(r)
            proj_ref[r * SUB_M:(r + 1) * SUB_M, :] = (y * _sigmoid(y)).astype(BF16)

    @pl.when(j >= N_Z_END)
    def _():
        st_scr[0:HALO, :] = halo_scr[t]
        for r in range(TM_IN // SUB_M):
            st_scr[HALO + r * SUB_M:HALO + (r + 1) * SUB_M, :] = sub_dot(r)
            acc = cb_ref[...]
            for k in range(SSM_CONV):
                off = HALO - (SSM_CONV - 1) + k + r * SUB_M
                acc = acc + cw_ref[k:k + 1, :] * st_scr[off:off + SUB_M, :]
            proj_ref[r * SUB_M:(r + 1) * SUB_M, :] = (acc * _sigmoid(acc)).astype(BF16)
        halo_scr[t] = st_scr[TM_IN:TM_IN + HALO, :]


def _in_proj(x, g, w, wdt, conv_w, conv_b):
    xbc_tile = lambda i, j: (0, jnp.maximum(j - N_Z_END, 0))
    return pl.pallas_call(
        _in_proj_kernel,
        grid=(SEQ // TM_IN, D_PROJ // TN_IN),
        in_specs=[
            pl.BlockSpec((TM_IN, D_MODEL), lambda i, j: (i, 0)),
            pl.BlockSpec((1, D_MODEL), lambda i, j: (0, 0)),
            pl.BlockSpec((D_MODEL, TN_IN), lambda i, j: (0, j)),
            pl.BlockSpec((D_MODEL, LANES), lambda i, j: (0, 0)),
            pl.BlockSpec((SSM_CONV, TN_IN), xbc_tile),
            pl.BlockSpec((1, TN_IN), xbc_tile),
        ],
        out_specs=[
            pl.BlockSpec((TM_IN, TN_IN), lambda i, j: (i, j)),
            pl.BlockSpec((TM_IN, LANES), lambda i, j: (i, 0)),
        ],
        out_shape=[
            jax.ShapeDtypeStruct((SEQ, D_PROJ), BF16),
            jax.ShapeDtypeStruct((SEQ, LANES), F32),
        ],
        scratch_shapes=[
            pltpu.VMEM((TM_IN, D_MODEL), BF16),
            pltpu.VMEM((TM_IN + HALO, TN_IN), F32),
            pltpu.VMEM((D_XBC // TN_IN, HALO, TN_IN), F32),
        ],
        compiler_params=_params("arbitrary", "arbitrary"),
        name="in_proj",
    )(x, g, w, wdt, conv_w, conv_b)


def _gmlp_kernel(u_ref, v_ref, lng_ref, lnb_ref, ws_ref, bs_ref, ng_ref, ya_ref, wsm_scr, acc_scr):
    @pl.when(pl.program_id(0) == 0)
    def _():
        row = lax.broadcasted_iota(jnp.int32, (CHUNK, CHUNK), 0)
        col = lax.broadcasted_iota(jnp.int32, (CHUNK, CHUNK), 1)
        for g in range(N_GROUPS_A):
            wsm_scr[g] = jnp.where(row >= col, ws_ref[g], 0.0).astype(BF16)

    def chunk_body(c, carry):
        r0 = pl.multiple_of(c * CHUNK, CHUNK)
        rows = pl.ds(r0, CHUNK)
        ssq = jnp.zeros((CHUNK, 1), F32)
        for g in range(N_GROUPS_A):
            cols = slice(g * D_HEAD_A, (g + 1) * D_HEAD_A)
            v = v_ref[rows, cols].astype(F32)
            mu = jnp.mean(v, axis=-1, keepdims=True)
            vc = v - mu
            var = jnp.mean(vc * vc, axis=-1, keepdims=True)
            vn = (vc * lax.rsqrt(var + EPS) * lng_ref[:, cols] + lnb_ref[:, cols]).astype(BF16)
            sg = jnp.dot(wsm_scr[g], vn, preferred_element_type=F32) + bs_ref[g]
            ya = u_ref[rows, cols].astype(F32) * sg
            ssq = ssq + jnp.sum(ya * ya, axis=-1, keepdims=True)
            acc_scr[rows, cols] = ya
        inv = lax.rsqrt(ssq * (1.0 / D_A) + EPS)
        ya_ref[rows, :] = (acc_scr[rows, :] * inv * ng_ref[...]).astype(BF16)
        return carry

    lax.fori_loop(0, TM_GMLP // CHUNK, chunk_body, 0)


def _gmlp(proj, ln_g, ln_b, w_s, bs_full, norm_g):
    return pl.pallas_call(
        _gmlp_kernel,
        grid=(SEQ // TM_GMLP,),
        in_specs=[
            pl.BlockSpec((TM_GMLP, D_A), lambda i: (i, 0)),
            pl.BlockSpec((TM_GMLP, D_A), lambda i: (i, 1)),
            pl.BlockSpec((1, D_A), lambda i: (0, 0)),
            pl.BlockSpec((1, D_A), lambda i: (0, 0)),
            pl.BlockSpec((N_GROUPS_A, CHUNK, CHUNK), lambda i: (0, 0, 0)),
            pl.BlockSpec((N_GROUPS_A, CHUNK, D_HEAD_A), lambda i: (0, 0, 0)),
            pl.BlockSpec((1, D_A), lambda i: (0, 0)),
        ],
        out_specs=pl.BlockSpec((TM_GMLP, D_A), lambda i: (i, 0)),
        out_shape=jax.ShapeDtypeStruct((SEQ, D_A), BF16),
        scratch_shapes=[
            pltpu.VMEM((N_GROUPS_A, CHUNK, CHUNK), BF16),
            pltpu.VMEM((TM_GMLP, D_A), F32),
        ],
        compiler_params=_params("arbitrary"),
        name="gmlp",
    )(proj, proj, ln_g, ln_b, w_s, bs_full, norm_g)


def _ssd_kernel(zs_ref, xs_ref, bc_ref, dt_ref, dtb_ref, alog_ref, dskip_ref, ng_ref, yb_ref,
                h_scr, cs2_scr, srct_scr, w2t_scr):
    @pl.when(pl.program_id(0) == 0)
    def _():
        h_scr[...] = jnp.zeros((D_STATE, D_SSM), F32)

    row = lax.broadcasted_iota(jnp.int32, (CHUNK, CHUNK), 0)
    lane = lax.broadcasted_iota(jnp.int32, (CHUNK, CHUNK), 1)
    tril = row >= lane
    tri_b = jnp.where(tril, 1.0, 0.0).astype(BF16)
    left = lane < SSM_HEAD_DIM
    mask_l = jnp.where(left, 1.0, 0.0).astype(BF16)
    mask_r = jnp.where(left, 0.0, 1.0).astype(BF16)
    a_neg = -jnp.exp(alog_ref[...])

    dtv = dt_ref[...] + dtb_ref[...]
    dt_all = jnp.maximum(dtv, 0.0) + jnp.log1p(jnp.exp(-jnp.abs(dtv)))
    for c in range(R_SSD // CHUNK):
        crows = slice(c * CHUNK, (c + 1) * CHUNK)
        dt = dt_all[crows, :]
        adt = dt * a_neg
        hi = adt.astype(BF16)
        r1 = adt - hi.astype(F32)
        mid = r1.astype(BF16)
        lo = (r1 - mid.astype(F32)).astype(BF16)
        cs = (jnp.dot(tri_b, hi, preferred_element_type=F32)
              + jnp.dot(tri_b, mid, preferred_element_type=F32)
              + jnp.dot(tri_b, lo, preferred_element_type=F32))
        cs2 = cs * LOG2E
        cs2_scr[crows, :] = cs2
        srct_scr[crows, :] = (cs2 - jnp.log2(dt)).T
        w2t_scr[crows, :] = (dt * jnp.exp2(cs2[CHUNK - 1:CHUNK, :] - cs2)).T

    def chunk_body(c, carry):
        r0 = pl.multiple_of(c * CHUNK, CHUNK)
        rows = pl.ds(r0, CHUNK)
        cs2 = cs2_scr[rows, :]
        src_t = srct_scr[rows, :]
        w2_t = w2t_scr[rows, :]

        for g in range(N_SSM_GROUPS):
            b_g = bc_ref[rows, g * D_STATE:(g + 1) * D_STATE]
            c_g = bc_ref[rows, N_SSM_GROUPS * D_STATE + g * D_STATE:
                         N_SSM_GROUPS * D_STATE + (g + 1) * D_STATE]
            c_f = c_g.astype(F32)
            bt_f = b_g.astype(F32).T
            scores = jnp.dot(c_g, bt_f.astype(BF16), preferred_element_type=F32)

            ys = []
            ssq = jnp.zeros((CHUNK, 1), F32)
            for jp in range(2):
                pcols = slice((2 * g + jp) * LANES, (2 * g + jp + 1) * LANES)
                xp = xs_ref[rows, pcols]
                hp = h_scr[:, pcols]
                hpb = hp.astype(BF16)
                lhs_y, lhs_s, decay = [], [], []
                for hh in range(2):
                    h = g * HEADS_PER_GROUP + 2 * jp + hh
                    colv = jnp.sum(jnp.where(lane == h, cs2, 0.0), axis=-1, keepdims=True)
                    lmat = jnp.exp2(jnp.where(tril, colv - src_t[h:h + 1, :], -jnp.inf))
                    ecol = jnp.exp2(colv)
                    lhs_y += [(scores * lmat).astype(BF16), (c_f * ecol).astype(BF16)]
                    lhs_s.append((bt_f * w2_t[h:h + 1, :]).astype(BF16))
                    decay.append(jnp.exp2(colv[CHUNK - 1:CHUNK, :]))
                x_l, x_r = xp * mask_l, xp * mask_r
                yd = jnp.dot(jnp.concatenate(lhs_y, axis=1),
                             jnp.concatenate([x_l, hpb * mask_l, x_r, hpb * mask_r], axis=0),
                             preferred_element_type=F32)
                states = jnp.dot(jnp.concatenate(lhs_s, axis=1),
                                 jnp.concatenate([x_l, x_r], axis=0),
                                 preferred_element_type=F32)
                h_scr[:, pcols] = jnp.where(left, decay[0], decay[1]) * hp + states

                y = yd + dskip_ref[:, pcols] * xp.astype(F32)
                y = y * zs_ref[rows, pcols].astype(F32)
                ssq = ssq + jnp.sum(y * y, axis=-1, keepdims=True)
                ys.append((pcols, y))
            inv = lax.rsqrt(ssq * (1.0 / (2 * LANES)) + EPS)
            for pcols, y in ys:
                yb_ref[rows, pcols] = (y * inv * ng_ref[:, pcols]).astype(BF16)
        return carry

    lax.fori_loop(0, R_SSD // CHUNK, chunk_body, 0)


def _ssd(proj, dt_raw, dt_bias, a_log, d_skip_e, norm_g):
    blk = lambda col: pl.BlockSpec((R_SSD, D_SSM), lambda i: (i, col))
    full = lambda shape: pl.BlockSpec(shape, lambda i: (0,) * len(shape))
    return pl.pallas_call(
        _ssd_kernel,
        grid=(SEQ // R_SSD,),
        in_specs=[
            blk(2), blk(3), blk(4),
            pl.BlockSpec((R_SSD, LANES), lambda i: (i, 0)),
            full((1, LANES)), full((1, LANES)), full((1, D_SSM)), full((1, D_SSM)),
        ],
        out_specs=pl.BlockSpec((R_SSD, D_SSM), lambda i: (i, 0)),
        out_shape=jax.ShapeDtypeStruct((SEQ, D_SSM), BF16),
        scratch_shapes=[pltpu.VMEM((D_STATE, D_SSM), F32)] + [pltpu.VMEM((R_SSD, LANES), F32)] * 3,
        compiler_params=_params("arbitrary"),
        name="ssd",
    )(proj, proj, proj, dt_raw, dt_bias, a_log, d_skip_e, norm_g)


def _proj_res_norm_kernel(n_lhs, tn, *refs):
    lhs_refs = refs[:n_lhs]
    w_ref, res_ref, g_ref, h_ref, n_ref = refs[n_lhs:]
    j = pl.program_id(1)
    k0 = 0
    acc = res_ref[...]
    for lhs in lhs_refs:
        k1 = k0 + lhs.shape[1]
        acc = acc + jnp.dot(lhs[...], w_ref[k0:k1, :], preferred_element_type=F32)
        k0 = k1
    h_ref[:, pl.ds(pl.multiple_of(j * tn, tn), tn)] = acc

    @pl.when(j == pl.num_programs(1) - 1)
    def _():
        h = h_ref[...]
        n_ref[...] = (h * _rms_scale(h) * g_ref[...]).astype(BF16)


def _proj_res_norm(lhs_list, w, res, g, tm, tn, name):
    k_total = sum(l.shape[1] for l in lhs_list)
    n_out = w.shape[1]
    in_specs = [pl.BlockSpec((tm, l.shape[1]), lambda i, j: (i, 0)) for l in lhs_list]
    in_specs += [
        pl.BlockSpec((k_total, tn), lambda i, j: (0, j)),
        pl.BlockSpec((tm, tn), lambda i, j: (i, j)),
        pl.BlockSpec((1, n_out), lambda i, j: (0, 0)),
    ]
    return pl.pallas_call(
        functools.partial(_proj_res_norm_kernel, len(lhs_list), tn),
        grid=(SEQ // tm, n_out // tn),
        in_specs=in_specs,
        out_specs=[
            pl.BlockSpec((tm, n_out), lambda i, j: (i, 0)),
            pl.BlockSpec((tm, n_out), lambda i, j: (i, 0)),
        ],
        out_shape=[
            jax.ShapeDtypeStruct((SEQ, n_out), F32),
            jax.ShapeDtypeStruct((SEQ, n_out), BF16),
        ],
        compiler_params=_params("arbitrary", "arbitrary"),
        name=name,
    )(*lhs_list, w, res, g)


def _resident_proj_res_norm_kernel(n_lhs, tm, *refs):
    lhs_refs = refs[:n_lhs]
    w_ref, res_ref, g_ref, h_ref, n_ref = refs[n_lhs:]
    for r in range(tm // SUB_M):
        rows = slice(r * SUB_M, (r + 1) * SUB_M)
        h = res_ref[rows, :]
        k0 = 0
        for lhs in lhs_refs:
            k1 = k0 + lhs.shape[1]
            h = h + jnp.dot(lhs[rows, :], w_ref[k0:k1, :], preferred_element_type=F32)
            k0 = k1
        h_ref[rows, :] = h
        n_ref[rows, :] = (h * _rms_scale(h) * g_ref[...]).astype(BF16)


def _resident_proj_res_norm(lhs_list, w, res, g, tm, name):
    n_out = w.shape[1]
    in_specs = [pl.BlockSpec((tm, l.shape[1]), lambda i: (i, 0)) for l in lhs_list]
    in_specs += [
        pl.BlockSpec(w.shape, lambda i: (0, 0), pipeline_mode=pl.Buffered(1)),
        pl.BlockSpec((tm, n_out), lambda i: (i, 0)),
        pl.BlockSpec((1, n_out), lambda i: (0, 0)),
    ]
    return pl.pallas_call(
        functools.partial(_resident_proj_res_norm_kernel, len(lhs_list), tm),
        grid=(SEQ // tm,),
        in_specs=in_specs,
        out_specs=[
            pl.BlockSpec((tm, n_out), lambda i: (i, 0)),
            pl.BlockSpec((tm, n_out), lambda i: (i, 0)),
        ],
        out_shape=[
            jax.ShapeDtypeStruct((SEQ, n_out), F32),
            jax.ShapeDtypeStruct((SEQ, n_out), BF16),
        ],
        compiler_params=_params("arbitrary"),
        name=name,
    )(*lhs_list, w, res, g)


def _up_kernel(f_ref, wg_ref, wu_ref, cwg_ref, cwu_ref, cbg_ref, cbu_ref, act_ref, st_scr, halo_scr):
    i = pl.program_id(0)
    j = pl.program_id(1)

    @pl.when(i == 0)
    def _():
        halo_scr[j] = jnp.zeros((HALO, 2 * TN_UP), F32)

    f = f_ref[...]
    st_scr[0:HALO, :] = halo_scr[j]
    st_scr[HALO:HALO + TM_UP, 0:TN_UP] = jnp.dot(f, wg_ref[...], preferred_element_type=F32)
    st_scr[HALO:HALO + TM_UP, TN_UP:2 * TN_UP] = jnp.dot(f, wu_ref[...], preferred_element_type=F32)
    halo_scr[j] = st_scr[TM_UP:TM_UP + HALO, :]

    def conv(cols, cw_ref, cb_ref):
        acc = cb_ref[...]
        for k in range(FFN_CONV):
            off = HALO - (FFN_CONV - 1) + k
            acc = acc + cw_ref[k:k + 1, :] * st_scr[off:off + TM_UP, cols]
        return acc

    gate = conv(slice(0, TN_UP), cwg_ref, cbg_ref)
    up = conv(slice(TN_UP, 2 * TN_UP), cwu_ref, cbu_ref)
    act_ref[...] = (gate * _sigmoid(gate) * up).astype(BF16)


def _up_proj(f, w_up, conv_w, conv_b):
    n_t = D_FF // TN_UP
    return pl.pallas_call(
        _up_kernel,
        grid=(SEQ // TM_UP, n_t),
        in_specs=[
            pl.BlockSpec((TM_UP, D_MODEL), lambda i, j: (i, 0)),
            pl.BlockSpec((D_MODEL, TN_UP), lambda i, j: (0, j)),
            pl.BlockSpec((D_MODEL, TN_UP), lambda i, j: (0, j + n_t)),
            pl.BlockSpec((FFN_CONV, TN_UP), lambda i, j: (0, j)),
            pl.BlockSpec((FFN_CONV, TN_UP), lambda i, j: (0, j + n_t)),
            pl.BlockSpec((1, TN_UP), lambda i, j: (0, j)),
            pl.BlockSpec((1, TN_UP), lambda i, j: (0, j + n_t)),
        ],
        out_specs=pl.BlockSpec((TM_UP, TN_UP), lambda i, j: (i, j)),
        out_shape=jax.ShapeDtypeStruct((SEQ, D_FF), BF16),
        scratch_shapes=[
            pltpu.VMEM((TM_UP + HALO, 2 * TN_UP), F32),
            pltpu.VMEM((n_t, HALO, 2 * TN_UP), F32),
        ],
        compiler_params=_params("arbitrary", "arbitrary"),
        name="up_proj",
    )(f, w_up, w_up, conv_w, conv_w, conv_b, conv_b)


def _ple_kernel(n_ref, wpg_ref, p_ref, wple_ref, h_ref, g_ref, o_ref):
    gate = _sigmoid(jnp.dot(n_ref[...], wpg_ref[...], preferred_element_type=F32))
    emb = jnp.dot(p_ref[...].astype(BF16), wple_ref[...], preferred_element_type=F32)
    h = h_ref[...] + gate * emb
    o_ref[...] = h * _rms_scale(h) * g_ref[...]


def _ple(n3, w_pg, p, w_ple, h2, g):
    return pl.pallas_call(
        _ple_kernel,
        grid=(SEQ // TM_PLE,),
        in_specs=[
            pl.BlockSpec((TM_PLE, D_MODEL), lambda i: (i, 0)),
            pl.BlockSpec((D_MODEL, D_MODEL), lambda i: (0, 0)),
            pl.BlockSpec((TM_PLE, D_PLE), lambda i: (i, 0)),
            pl.BlockSpec((D_PLE, D_MODEL), lambda i: (0, 0)),
            pl.BlockSpec((TM_PLE, D_MODEL), lambda i: (i, 0)),
            pl.BlockSpec((1, D_MODEL), lambda i: (0, 0)),
        ],
        out_specs=pl.BlockSpec((TM_PLE, D_MODEL), lambda i: (i, 0)),
        out_shape=jax.ShapeDtypeStruct((SEQ, D_MODEL), F32),
        compiler_params=_params("arbitrary"),
        name="ple",
    )(n3, w_pg, p, w_ple, h2, g)


def _layer(h, p_i, norm_mix_g, w_in, ln_a_g, ln_a_b, w_s, b_s, norm_a_g, conv_ssm_w, conv_ssm_b,
           dt_bias, a_log, d_skip, ssm_norm_g, w_out, norm_ffn_g, w_up, conv_ffn_w, conv_ffn_b,
           w_down, norm_ple_g, w_ple_gate, w_ple):
    row = lambda v: v.reshape(1, -1)
    pad_lanes = lambda v: jnp.pad(v, (0, LANES - v.shape[0])).reshape(1, LANES)

    w_dt = jnp.pad(w_in[:, D_PROJ:], ((0, 0), (0, LANES - N_SSM_HEADS))).astype(BF16)
    proj, dt_raw = _in_proj(h, row(norm_mix_g), w_in.astype(BF16), w_dt, conv_ssm_w, row(conv_ssm_b))

    bs_full = jnp.broadcast_to(b_s[:, :, None], (N_GROUPS_A, CHUNK, D_HEAD_A))
    y_a = _gmlp(proj, row(ln_a_g), row(ln_a_b), w_s, bs_full, row(norm_a_g))

    y_b = _ssd(proj, dt_raw, pad_lanes(dt_bias), pad_lanes(a_log),
               row(jnp.repeat(d_skip, SSM_HEAD_DIM)), row(ssm_norm_g))

    h1, f = _resident_proj_res_norm([y_a, y_b], w_out.astype(BF16), h, row(norm_ffn_g),
                                    TM_OUT, "out_proj")
    act = _up_proj(f, w_up.astype(BF16), conv_ffn_w, row(conv_ffn_b))
    h2, n3 = _proj_res_norm([act], w_down.astype(BF16), h1, row(norm_ple_g),
                            TM_DOWN, TN_DOWN, "down_proj")
    return h2, n3, w_ple_gate.astype(BF16), p_i, w_ple.astype(BF16)


def kernel(x, p, norm_mix_g, w_in, ln_a_g, ln_a_b, w_s, b_s, norm_a_g, conv_ssm_w, conv_ssm_b,
           dt_bias, a_log, d_skip, ssm_norm_g, w_out, norm_ffn_g, w_up, conv_ffn_w, conv_ffn_b,
           w_down, norm_ple_g, w_ple_gate, w_ple, norm_final_g):
    assert x.shape == (1, SEQ, D_MODEL) and w_in.shape[0] == 1
    h = x[0]
    h2, n3, w_pg, p_i, w_pl = _layer(
        h, p[0, 0], norm_mix_g[0], w_in[0], ln_a_g[0], ln_a_b[0], w_s[0], b_s[0], norm_a_g[0],
        conv_ssm_w[0], conv_ssm_b[0], dt_bias[0], a_log[0], d_skip[0], ssm_norm_g[0], w_out[0],
        norm_ffn_g[0], w_up[0], conv_ffn_w[0], conv_ffn_b[0], w_down[0], norm_ple_g[0],
        w_ple_gate[0], w_ple[0])
    out = _ple(n3, w_pg, p_i, w_pl, h2, norm_final_g.reshape(1, -1))
    return out[None]
```

```python
import functools
import math

import jax
import jax.numpy as jnp
from jax import lax
from jax.experimental import pallas as pl
from jax.experimental.pallas import tpu as pltpu

F32 = jnp.float32
BF16 = jnp.bfloat16

D_MODEL = 2048
SEQ = 8192
D_A = 2048
CHUNK = 128
N_GROUPS_A = 16
D_HEAD_A = 128
D_SSM = 2048
SSM_HEAD_DIM = 64
N_SSM_HEADS = 32
N_SSM_GROUPS = 8
HEADS_PER_GROUP = 4
D_STATE = 128
SSM_CONV = 4
D_XBC = 4096
D_PROJ = 10240
D_FF = 5632
FFN_CONV = 3
D_PLE = 256
EPS = 1e-6
LOG2E = 1.0 / math.log(2.0)

LANES = 128
SUBLANES = 8
HALO = SUBLANES
VMEM_LIMIT_BYTES = 56 * 1024 * 1024

TM_IN = 1024
TN_IN = 1024
N_UV_TILES = (2 * D_A) // TN_IN
N_Z_END = (2 * D_A + D_SSM) // TN_IN
SUB_M = 256
TM_GMLP = 512
R_SSD = 512
TM_OUT = 512
TN_OUT = 512
TM_UP = 1024
SUB_UP = 1024
TN_UP = 512
TM_DOWN = 512
TN_DOWN = 512
TM_PLE = 512


def _params(*sem):
    return pltpu.CompilerParams(dimension_semantics=sem, vmem_limit_bytes=VMEM_LIMIT_BYTES)


def _sigmoid(x):
    return 1.0 / (1.0 + jnp.exp(-x))


def _rms_scale(x):
    return lax.rsqrt(jnp.mean(x * x, axis=-1, keepdims=True) + EPS)


def _in_proj_kernel(x_ref, g_ref, w_ref, wdt_ref, cw_ref, cb_ref, proj_ref, dt_ref,
                    a_scr, halo_scr):
    i = pl.program_id(0)
    j = pl.program_id(1)
    t = jnp.maximum(j - N_Z_END, 0)

    @pl.when(j == 0)
    def _():
        x = x_ref[...]
        a = (x * _rms_scale(x) * g_ref[...]).astype(BF16)
        a_scr[...] = a
        dt_ref[...] = jnp.dot(a, wdt_ref[...], preferred_element_type=F32)

    @pl.when((i == 0) & (j >= N_Z_END))
    def _():
        halo_scr[t] = jnp.zeros((HALO, TN_IN), F32)

    @pl.when(j < N_UV_TILES)
    def _():
        y = jnp.dot(a_scr[...], w_ref[...], preferred_element_type=F32)
        proj_ref[...] = jax.nn.gelu(y).astype(BF16)

    @pl.when((j >= N_UV_TILES) & (j < N_Z_END))
    def _():
        y = jnp.dot(a_scr[...], w_ref[...], preferred_element_type=F32)
        proj_ref[...] = (y * _sigmoid(y)).astype(BF16)

    @pl.when(j >= N_Z_END)
    def _():
        y = jnp.dot(a_scr[...], w_ref[...], preferred_element_type=F32)
        full = jnp.concatenate([halo_scr[t], y], axis=0)
        halo_scr[t] = y[TM_IN - HALO:, :]
        acc = cb_ref[...] + cw_ref[SSM_CONV - 1:SSM_CONV, :] * y
        for k in range(SSM_CONV - 1):
            shifted = pltpu.roll(full, SSM_CONV - 1 - k, axis=0)[HALO:, :]
            acc = acc + cw_ref[k:k + 1, :] * shifted
        proj_ref[...] = (acc * _sigmoid(acc)).astype(BF16)


def _in_proj(x, g, w, wdt, conv_w, conv_b):
    xbc_tile = lambda i, j: (0, jnp.maximum(j - N_Z_END, 0))
    return pl.pallas_call(
        _in_proj_kernel,
        grid=(SEQ // TM_IN, D_PROJ // TN_IN),
        in_specs=[
            pl.BlockSpec((TM_IN, D_MODEL), lambda i, j: (i, 0)),
            pl.BlockSpec((1, D_MODEL), lambda i, j: (0, 0)),
            pl.BlockSpec((D_MODEL, TN_IN), lambda i, j: (0, j)),
            pl.BlockSpec((D_MODEL, LANES), lambda i, j: (0, 0)),
            pl.BlockSpec((SSM_CONV, TN_IN), xbc_tile),
            pl.BlockSpec((1, TN_IN), xbc_tile),
        ],
        out_specs=[
            pl.BlockSpec((TM_IN, TN_IN), lambda i, j: (i, j)),
            pl.BlockSpec((TM_IN, LANES), lambda i, j: (i, 0)),
        ],
        out_shape=[
            jax.ShapeDtypeStruct((SEQ, D_PROJ), BF16),
            jax.ShapeDtypeStruct((SEQ, LANES), F32),
        ],
        scratch_shapes=[
            pltpu.VMEM((TM_IN, D_MODEL), BF16),
            pltpu.VMEM((D_XBC // TN_IN, HALO, TN_IN), F32),
        ],
        compiler_params=_params("arbitrary", "arbitrary"),
        name="in_proj",
    )(x, g, w, wdt, conv_w, conv_b)


def _gmlp_kernel(u_ref, v_ref, lng_ref, lnb_ref, ws_ref, bs_ref, ng_ref, ya_ref, wsm_scr, acc_scr):
    @pl.when(pl.program_id(0) == 0)
    def _():
        row = lax.broadcasted_iota(jnp.int32, (CHUNK, CHUNK), 0)
        col = lax.broadcasted_iota(jnp.int32, (CHUNK, CHUNK), 1)
        for g in range(N_GROUPS_A):
            wsm_scr[g] = jnp.where(row >= col, ws_ref[g], 0.0).astype(BF16)

    gi = lax.broadcasted_iota(jnp.int32, (2 * D_HEAD_A, 2 * D_HEAD_A), 0) // D_HEAD_A
    gj = lax.broadcasted_iota(jnp.int32, (2 * D_HEAD_A, 2 * D_HEAD_A), 1) // D_HEAD_A
    mean_m = jnp.where(gi == gj, 1.0 / D_HEAD_A, 0.0).astype(BF16)

    def group_mean(x_hi, x_lo=None):
        m = jnp.dot(x_hi, mean_m, preferred_element_type=F32)
        if x_lo is not None:
            m = m + jnp.dot(x_lo, mean_m, preferred_element_type=F32)
        return m

    def chunk_body(c, carry):
        r0 = pl.multiple_of(c * CHUNK, CHUNK)
        rows = pl.ds(r0, CHUNK)
        pairs = [slice(gp * 2 * D_HEAD_A, (gp + 1) * 2 * D_HEAD_A) for gp in range(N_GROUPS_A // 2)]
        vbs = [v_ref[rows, cols2] for cols2 in pairs]
        vcs = [vb.astype(F32) - group_mean(vb) for vb in vbs]
        sqs = [vc * vc for vc in vcs]
        sq_his = [sq.astype(BF16) for sq in sqs]
        vars_ = [group_mean(hi, (sq - hi.astype(F32)).astype(BF16)) for sq, hi in zip(sqs, sq_his)]
        vns = [(vc * lax.rsqrt(var + EPS) * lng_ref[:, cols2] + lnb_ref[:, cols2]).astype(BF16)
               for vc, var, cols2 in zip(vcs, vars_, pairs)]
        ssq = jnp.zeros((CHUNK, D_HEAD_A), F32)
        for g in range(N_GROUPS_A):
            cols = slice(g * D_HEAD_A, (g + 1) * D_HEAD_A)
            vn = vns[g // 2][:, (g % 2) * D_HEAD_A:(g % 2 + 1) * D_HEAD_A]
            sg = jnp.dot(wsm_scr[g], vn, preferred_element_type=F32) + bs_ref[g]
            ya = u_ref[rows, cols].astype(F32) * sg
            ssq = ssq + ya * ya
            acc_scr[rows, cols] = ya
        inv = lax.rsqrt(jnp.sum(ssq, axis=-1, keepdims=True) * (1.0 / D_A) + EPS)
        ya_ref[rows, :] = (acc_scr[rows, :] * inv * ng_ref[...]).astype(BF16)
        return carry

    lax.fori_loop(0, TM_GMLP // CHUNK, chunk_body, 0)


def _gmlp(proj, ln_g, ln_b, w_s, bs_full, norm_g):
    return pl.pallas_call(
        _gmlp_kernel,
        grid=(SEQ // TM_GMLP,),
        in_specs=[
            pl.BlockSpec((TM_GMLP, D_A), lambda i: (i, 0)),
            pl.BlockSpec((TM_GMLP, D_A), lambda i: (i, 1)),
            pl.BlockSpec((1, D_A), lambda i: (0, 0)),
            pl.BlockSpec((1, D_A), lambda i: (0, 0)),
            pl.BlockSpec((N_GROUPS_A, CHUNK, CHUNK), lambda i: (0, 0, 0)),
            pl.BlockSpec((N_GROUPS_A, CHUNK, D_HEAD_A), lambda i: (0, 0, 0)),
            pl.BlockSpec((1, D_A), lambda i: (0, 0)),
        ],
        out_specs=pl.BlockSpec((TM_GMLP, D_A), lambda i: (i, 0)),
        out_shape=jax.ShapeDtypeStruct((SEQ, D_A), BF16),
        scratch_shapes=[
            pltpu.VMEM((N_GROUPS_A, CHUNK, CHUNK), BF16),
            pltpu.VMEM((TM_GMLP, D_A), F32),
        ],
        compiler_params=_params("arbitrary"),
        name="gmlp",
    )(proj, proj, ln_g, ln_b, w_s, bs_full, norm_g)


def _ssd_kernel(zs_ref, xs_ref, bc_ref, dt_ref, dtb_ref, alog_ref, dskip_ref, ng_ref, yb_ref,
                h_scr, cs2_scr, srct_scr, w2t_scr):
    @pl.when(pl.program_id(0) == 0)
    def _():
        h_scr[...] = jnp.zeros((D_STATE, D_SSM), F32)

    row = lax.broadcasted_iota(jnp.int32, (CHUNK, CHUNK), 0)
    lane = lax.broadcasted_iota(jnp.int32, (CHUNK, CHUNK), 1)
    tril = row >= lane
    tri_b = jnp.where(tril, 1.0, 0.0).astype(BF16)
    left = lane < SSM_HEAD_DIM
    mask_l = jnp.where(left, 1.0, 0.0).astype(BF16)
    mask_r = jnp.where(left, 0.0, 1.0).astype(BF16)
    a_neg = -jnp.exp(alog_ref[...])

    dtv = dt_ref[...] + dtb_ref[...]
    dt_all = jnp.maximum(dtv, 0.0) + jnp.log1p(jnp.exp(-jnp.abs(dtv)))
    for c in range(R_SSD // CHUNK):
        crows = slice(c * CHUNK, (c + 1) * CHUNK)
        dt = dt_all[crows, :]
        adt = dt * a_neg
        hi = adt.astype(BF16)
        r1 = adt - hi.astype(F32)
        mid = r1.astype(BF16)
        lo = (r1 - mid.astype(F32)).astype(BF16)
        cs = (jnp.dot(tri_b, hi, preferred_element_type=F32)
              + jnp.dot(tri_b, mid, preferred_element_type=F32)
              + jnp.dot(tri_b, lo, preferred_element_type=F32))
        cs2 = cs * LOG2E
        cs2_scr[crows, :] = cs2
        srct_scr[crows, :] = (cs2 - jnp.log2(dt)).T
        w2t_scr[crows, :] = (dt * jnp.exp2(cs2[CHUNK - 1:CHUNK, :] - cs2)).T

    def chunk_body(c, carry):
        r0 = pl.multiple_of(c * CHUNK, CHUNK)
        rows = pl.ds(r0, CHUNK)
        cs2 = cs2_scr[rows, :]
        src_t = srct_scr[rows, :]
        w2_t = w2t_scr[rows, :]

        for g in range(N_SSM_GROUPS):
            b_g = bc_ref[rows, g * D_STATE:(g + 1) * D_STATE]
            c_g = bc_ref[rows, N_SSM_GROUPS * D_STATE + g * D_STATE:
                         N_SSM_GROUPS * D_STATE + (g + 1) * D_STATE]
            c_f = c_g.astype(F32)
            bt_f = b_g.astype(F32).T
            scores = jnp.dot(c_g, bt_f.astype(BF16), preferred_element_type=F32)

            ys = []
            ssq = jnp.zeros((CHUNK, 1), F32)
            for jp in range(2):
                pcols = slice((2 * g + jp) * LANES, (2 * g + jp + 1) * LANES)
                xp = xs_ref[rows, pcols]
                hp = h_scr[:, pcols]
                hpb = hp.astype(BF16)
                lhs_y, lhs_s, decay = [], [], []
                for hh in range(2):
                    h = g * HEADS_PER_GROUP + 2 * jp + hh
                    colv = jnp.sum(jnp.where(lane == h, cs2, 0.0), axis=-1, keepdims=True)
                    lmat = jnp.exp2(jnp.where(tril, colv - src_t[h:h + 1, :], -jnp.inf))
                    ecol = jnp.exp2(colv)
                    lhs_y += [(scores * lmat).astype(BF16), (c_f * ecol).astype(BF16)]
                    lhs_s.append((bt_f * w2_t[h:h + 1, :]).astype(BF16))
                    decay.append(jnp.exp2(colv[CHUNK - 1:CHUNK, :]))
                x_l, x_r = xp * mask_l, xp * mask_r
                yd = jnp.dot(jnp.concatenate(lhs_y, axis=1),
                             jnp.concatenate([x_l, hpb * mask_l, x_r, hpb * mask_r], axis=0),
                             preferred_element_type=F32)
                states = jnp.dot(jnp.concatenate(lhs_s, axis=1),
                                 jnp.concatenate([x_l, x_r], axis=0),
                                 preferred_element_type=F32)
                h_scr[:, pcols] = jnp.where(left, decay[0], decay[1]) * hp + states

                y = yd + dskip_ref[:, pcols] * xp.astype(F32)
                y = y * zs_ref[rows, pcols].astype(F32)
                ssq = ssq + jnp.sum(y * y, axis=-1, keepdims=True)
                ys.append((pcols, y))
            inv = lax.rsqrt(ssq * (1.0 / (2 * LANES)) + EPS)
            for pcols, y in ys:
                yb_ref[rows, pcols] = (y * inv * ng_ref[:, pcols]).astype(BF16)
        return carry

    lax.fori_loop(0, R_SSD // CHUNK, chunk_body, 0)


def _ssd(proj, dt_raw, dt_bias, a_log, d_skip_e, norm_g):
    blk = lambda col: pl.BlockSpec((R_SSD, D_SSM), lambda i: (i, col))
    full = lambda shape: pl.BlockSpec(shape, lambda i: (0,) * len(shape))
    return pl.pallas_call(
        _ssd_kernel,
        grid=(SEQ // R_SSD,),
        in_specs=[
            blk(2), blk(3), blk(4),
            pl.BlockSpec((R_SSD, LANES), lambda i: (i, 0)),
            full((1, LANES)), full((1, LANES)), full((1, D_SSM)), full((1, D_SSM)),
        ],
        out_specs=pl.BlockSpec((R_SSD, D_SSM), lambda i: (i, 0)),
        out_shape=jax.ShapeDtypeStruct((SEQ, D_SSM), BF16),
        scratch_shapes=[pltpu.VMEM((D_STATE, D_SSM), F32)] + [pltpu.VMEM((R_SSD, LANES), F32)] * 3,
        compiler_params=_params("arbitrary"),
        name="ssd",
    )(proj, proj, proj, dt_raw, dt_bias, a_log, d_skip_e, norm_g)


def _proj_res_norm_kernel(n_lhs, tn, *refs):
    lhs_refs = refs[:n_lhs]
    w_ref, res_ref, g_ref, h_ref, n_ref = refs[n_lhs:]
    j = pl.program_id(1)
    k0 = 0
    acc = res_ref[...]
    for lhs in lhs_refs:
        k1 = k0 + lhs.shape[1]
        acc = acc + jnp.dot(lhs[...], w_ref[k0:k1, :], preferred_element_type=F32)
        k0 = k1
    h_ref[:, pl.ds(pl.multiple_of(j * tn, tn), tn)] = acc

    @pl.when(j == pl.num_programs(1) - 1)
    def _():
        h = h_ref[...]
        n_ref[...] = (h * _rms_scale(h) * g_ref[...]).astype(BF16)


def _proj_res_norm(lhs_list, w, res, g, tm, tn, name):
    k_total = sum(l.shape[1] for l in lhs_list)
    n_out = w.shape[1]
    in_specs = [pl.BlockSpec((tm, l.shape[1]), lambda i, j: (i, 0)) for l in lhs_list]
    in_specs += [
        pl.BlockSpec((k_total, tn), lambda i, j: (0, j)),
        pl.BlockSpec((tm, tn), lambda i, j: (i, j)),
        pl.BlockSpec((1, n_out), lambda i, j: (0, 0)),
    ]
    return pl.pallas_call(
        functools.partial(_proj_res_norm_kernel, len(lhs_list), tn),
        grid=(SEQ // tm, n_out // tn),
        in_specs=in_specs,
        out_specs=[
            pl.BlockSpec((tm, n_out), lambda i, j: (i, 0)),
            pl.BlockSpec((tm, n_out), lambda i, j: (i, 0)),
        ],
        out_shape=[
            jax.ShapeDtypeStruct((SEQ, n_out), F32),
            jax.ShapeDtypeStruct((SEQ, n_out), BF16),
        ],
        compiler_params=_params("arbitrary", "arbitrary"),
        name=name,
    )(*lhs_list, w, res, g)


def _resident_proj_res_norm_kernel(n_lhs, tm, *refs):
    lhs_refs = refs[:n_lhs]
    w_ref, res_ref, g_ref, h_ref, n_ref = refs[n_lhs:]
    for r in range(tm // SUB_M):
        rows = slice(r * SUB_M, (r + 1) * SUB_M)
        h = res_ref[rows, :]
        k0 = 0
        for lhs in lhs_refs:
            k1 = k0 + lhs.shape[1]
            h = h + jnp.dot(lhs[rows, :], w_ref[k0:k1, :], preferred_element_type=F32)
            k0 = k1
        h_ref[rows, :] = h
        n_ref[rows, :] = (h * _rms_scale(h) * g_ref[...]).astype(BF16)


def _resident_proj_res_norm(lhs_list, w, res, g, tm, name):
    n_out = w.shape[1]
    in_specs = [pl.BlockSpec((tm, l.shape[1]), lambda i: (i, 0)) for l in lhs_list]
    in_specs += [
        pl.BlockSpec(w.shape, lambda i: (0, 0), pipeline_mode=pl.Buffered(1)),
        pl.BlockSpec((tm, n_out), lambda i: (i, 0)),
        pl.BlockSpec((1, n_out), lambda i: (0, 0)),
    ]
    return pl.pallas_call(
        functools.partial(_resident_proj_res_norm_kernel, len(lhs_list), tm),
        grid=(SEQ // tm,),
        in_specs=in_specs,
        out_specs=[
            pl.BlockSpec((tm, n_out), lambda i: (i, 0)),
            pl.BlockSpec((tm, n_out), lambda i: (i, 0)),
        ],
        out_shape=[
            jax.ShapeDtypeStruct((SEQ, n_out), F32),
            jax.ShapeDtypeStruct((SEQ, n_out), BF16),
        ],
        compiler_params=_params("arbitrary"),
        name=name,
    )(*lhs_list, w, res, g)


def _up_kernel(f_ref, wg_ref, wu_ref, cwg_ref, cwu_ref, cbg_ref, cbu_ref, act_ref, halo_scr):
    i = pl.program_id(0)
    j = pl.program_id(1)

    @pl.when(i == 0)
    def _():
        halo_scr[j] = jnp.zeros((HALO, 2 * TN_UP), F32)

    def conv(f, hist, w_ref, cw_ref, cb_ref):
        y = jnp.dot(f, w_ref[...], preferred_element_type=F32)
        full = jnp.concatenate([hist, y], axis=0)
        acc = cb_ref[...] + cw_ref[FFN_CONV - 1:FFN_CONV, :] * y
        for k in range(FFN_CONV - 1):
            shifted = pltpu.roll(full, FFN_CONV - 1 - k, axis=0)[HALO:, :]
            acc = acc + cw_ref[k:k + 1, :] * shifted
        return acc, y[y.shape[0] - HALO:, :]

    gcols, ucols = slice(0, TN_UP), slice(TN_UP, 2 * TN_UP)
    hist_g, hist_u = halo_scr[j, :, gcols], halo_scr[j, :, ucols]
    for r in range(TM_UP // SUB_UP):
        rows = slice(r * SUB_UP, (r + 1) * SUB_UP)
        f = f_ref[rows, :]
        gate, hist_g = conv(f, hist_g, wg_ref, cwg_ref, cbg_ref)
        up, hist_u = conv(f, hist_u, wu_ref, cwu_ref, cbu_ref)
        act_ref[rows, :] = (gate * _sigmoid(gate) * up).astype(BF16)
    halo_scr[j, :, gcols] = hist_g
    halo_scr[j, :, ucols] = hist_u


def _up_proj(f, w_up, conv_w, conv_b):
    n_t = D_FF // TN_UP
    return pl.pallas_call(
        _up_kernel,
        grid=(SEQ // TM_UP, n_t),
        in_specs=[
            pl.BlockSpec((TM_UP, D_MODEL), lambda i, j: (i, 0)),
            pl.BlockSpec((D_MODEL, TN_UP), lambda i, j: (0, j)),
            pl.BlockSpec((D_MODEL, TN_UP), lambda i, j: (0, j + n_t)),
            pl.BlockSpec((FFN_CONV, TN_UP), lambda i, j: (0, j)),
            pl.BlockSpec((FFN_CONV, TN_UP), lambda i, j: (0, j + n_t)),
            pl.BlockSpec((1, TN_UP), lambda i, j: (0, j)),
            pl.BlockSpec((1, TN_UP), lambda i, j: (0, j + n_t)),
        ],
        out_specs=pl.BlockSpec((TM_UP, TN_UP), lambda i, j: (i, j)),
        out_shape=jax.ShapeDtypeStruct((SEQ, D_FF), BF16),
        scratch_shapes=[pltpu.VMEM((n_t, HALO, 2 * TN_UP), F32)],
        compiler_params=_params("arbitrary", "arbitrary"),
        name="up_proj",
    )(f, w_up, w_up, conv_w, conv_w, conv_b, conv_b)


def _ple_kernel(n_ref, wpg_ref, p_ref, wple_ref, h_ref, g_ref, o_ref):
    gate = _sigmoid(jnp.dot(n_ref[...], wpg_ref[...], preferred_element_type=F32))
    emb = jnp.dot(p_ref[...].astype(BF16), wple_ref[...], preferred_element_type=F32)
    h = h_ref[...] + gate * emb
    o_ref[...] = h * _rms_scale(h) * g_ref[...]


def _ple(n3, w_pg, p, w_ple, h2, g):
    return pl.pallas_call(
        _ple_kernel,
        grid=(SEQ // TM_PLE,),
        in_specs=[
            pl.BlockSpec((TM_PLE, D_MODEL), lambda i: (i, 0)),
            pl.BlockSpec((D_MODEL, D_MODEL), lambda i: (0, 0)),
            pl.BlockSpec((TM_PLE, D_PLE), lambda i: (i, 0)),
            pl.BlockSpec((D_PLE, D_MODEL), lambda i: (0, 0)),
            pl.BlockSpec((TM_PLE, D_MODEL), lambda i: (i, 0)),
            pl.BlockSpec((1, D_MODEL), lambda i: (0, 0)),
        ],
        out_specs=pl.BlockSpec((TM_PLE, D_MODEL), lambda i: (i, 0)),
        out_shape=jax.ShapeDtypeStruct((SEQ, D_MODEL), F32),
        compiler_params=_params("arbitrary"),
        name="ple",
    )(n3, w_pg, p, w_ple, h2, g)


def _layer(h, p_i, norm_mix_g, w_in, ln_a_g, ln_a_b, w_s, b_s, norm_a_g, conv_ssm_w, conv_ssm_b,
           dt_bias, a_log, d_skip, ssm_norm_g, w_out, norm_ffn_g, w_up, conv_ffn_w, conv_ffn_b,
           w_down, norm_ple_g, w_ple_gate, w_ple):
    row = lambda v: v.reshape(1, -1)
    pad_lanes = lambda v: jnp.pad(v, (0, LANES - v.shape[0])).reshape(1, LANES)

    w_dt = jnp.pad(w_in[:, D_PROJ:], ((0, 0), (0, LANES - N_SSM_HEADS))).astype(BF16)
    proj, dt_raw = _in_proj(h, row(norm_mix_g), w_in.astype(BF16), w_dt, conv_ssm_w, row(conv_ssm_b))

    bs_full = jnp.broadcast_to(b_s[:, :, None], (N_GROUPS_A, CHUNK, D_HEAD_A))
    y_a = _gmlp(proj, row(ln_a_g), row(ln_a_b), w_s, bs_full, row(norm_a_g))

    y_b = _ssd(proj, dt_raw, pad_lanes(dt_bias), pad_lanes(a_log),
               row(jnp.repeat(d_skip, SSM_HEAD_DIM)), row(ssm_norm_g))

    h1, f = _resident_proj_res_norm([y_a, y_b], w_out.astype(BF16), h, row(norm_ffn_g),
                                    TM_OUT, "out_proj")
    act = _up_proj(f, w_up.astype(BF16), conv_ffn_w, row(conv_ffn_b))
    h2, n3 = _proj_res_norm([act], w_down.astype(BF16), h1, row(norm_ple_g),
                            TM_DOWN, TN_DOWN, "down_proj")
    return h2, n3, w_ple_gate.astype(BF16), p_i, w_ple.astype(BF16)


def kernel(x, p, norm_mix_g, w_in, ln_a_g, ln_a_b, w_s, b_s, norm_a_g, conv_ssm_w, conv_ssm_b,
           dt_bias, a_log, d_skip, ssm_norm_g, w_out, norm_ffn_g, w_up, conv_ffn_w, conv_ffn_b,
           w_down, norm_ple_g, w_ple_gate, w_ple, norm_final_g):
    assert x.shape == (1, SEQ, D_MODEL) and w_in.shape[0] == 1
    h = x[0]
    h2, n3, w_pg, p_i, w_pl = _layer(
        h, p[0, 0], norm_mix_g[0], w_in[0], ln_a_g[0], ln_a_b[0], w_s[0], b_s[0], norm_a_g[0],
        conv_ssm_w[0], conv_ssm_b[0], dt_bias[0], a_log[0], d_skip[0], ssm_norm_g[0], w_out[0],
        norm_ffn_g[0], w_up[0], conv_ffn_w[0], conv_ffn_b[0], w_down[0], norm_ple_g[0],
        w_ple_gate[0], w_ple[0])
    out = _ple(n3, w_pg, p_i, w_pl, h2, norm_final_g.reshape(1, -1))
    return out[None]
```

```python
import functools
import math

import jax
import jax.numpy as jnp
from jax import lax
from jax.experimental import pallas as pl
from jax.experimental.pallas import tpu as pltpu

F32 = jnp.float32
BF16 = jnp.bfloat16

D_MODEL = 2048
SEQ = 8192
D_A = 2048
CHUNK = 128
N_GROUPS_A = 16
D_HEAD_A = 128
D_SSM = 2048
SSM_HEAD_DIM = 64
N_SSM_HEADS = 32
N_SSM_GROUPS = 8
HEADS_PER_GROUP = 4
D_STATE = 128
SSM_CONV = 4
D_XBC = 4096
D_PROJ = 10240
D_FF = 5632
FFN_CONV = 3
D_PLE = 256
EPS = 1e-6
LOG2E = 1.0 / math.log(2.0)

LANES = 128
SUBLANES = 8
HALO = SUBLANES
VMEM_LIMIT_BYTES = 56 * 1024 * 1024

TM_IN = 1024
TN_IN = 1024
N_UV_TILES = (2 * D_A) // TN_IN
N_Z_END = (2 * D_A + D_SSM) // TN_IN
SUB_M = 256
TM_GMLP = 512
R_SSD = 512
TM_OUT = 512
TM_UP = 1024
SUB_UP = 1024
TN_UP = 512
TM_DOWN = 256


def _params(*sem):
    return pltpu.CompilerParams(dimension_semantics=sem, vmem_limit_bytes=VMEM_LIMIT_BYTES)


def _sigmoid(x):
    return 1.0 / (1.0 + jnp.exp(-x))


def _rms_scale(x):
    return lax.rsqrt(jnp.mean(x * x, axis=-1, keepdims=True) + EPS)


def _in_proj_kernel(x_ref, g_ref, w_ref, wdt_ref, cw_ref, cb_ref, proj_ref, dt_ref,
                    a_scr, halo_scr):
    i = pl.program_id(0)
    j = pl.program_id(1)
    t = jnp.maximum(j - N_Z_END, 0)

    @pl.when(j == 0)
    def _():
        x = x_ref[...]
        a = (x * _rms_scale(x) * g_ref[...]).astype(BF16)
        a_scr[...] = a
        dt_ref[...] = jnp.dot(a, wdt_ref[...], preferred_element_type=F32)

    @pl.when((i == 0) & (j >= N_Z_END))
    def _():
        halo_scr[t] = jnp.zeros((HALO, TN_IN), F32)

    @pl.when(j < N_UV_TILES)
    def _():
        y = jnp.dot(a_scr[...], w_ref[...], preferred_element_type=F32)
        proj_ref[...] = jax.nn.gelu(y).astype(BF16)

    @pl.when((j >= N_UV_TILES) & (j < N_Z_END))
    def _():
        y = jnp.dot(a_scr[...], w_ref[...], preferred_element_type=F32)
        proj_ref[...] = (y * _sigmoid(y)).astype(BF16)

    @pl.when(j >= N_Z_END)
    def _():
        y = jnp.dot(a_scr[...], w_ref[...], preferred_element_type=F32)
        full = jnp.concatenate([halo_scr[t], y], axis=0)
        halo_scr[t] = y[TM_IN - HALO:, :]
        acc = cb_ref[...] + cw_ref[SSM_CONV - 1:SSM_CONV, :] * y
        for k in range(SSM_CONV - 1):
            shifted = pltpu.roll(full, SSM_CONV - 1 - k, axis=0)[HALO:, :]
            acc = acc + cw_ref[k:k + 1, :] * shifted
        proj_ref[...] = (acc * _sigmoid(acc)).astype(BF16)


def _in_proj(x, g, w, wdt, conv_w, conv_b):
    xbc_tile = lambda i, j: (0, jnp.maximum(j - N_Z_END, 0))
    return pl.pallas_call(
        _in_proj_kernel,
        grid=(SEQ // TM_IN, D_PROJ // TN_IN),
        in_specs=[
            pl.BlockSpec((TM_IN, D_MODEL), lambda i, j: (i, 0)),
            pl.BlockSpec((1, D_MODEL), lambda i, j: (0, 0)),
            pl.BlockSpec((D_MODEL, TN_IN), lambda i, j: (0, j)),
            pl.BlockSpec((D_MODEL, LANES), lambda i, j: (0, 0)),
            pl.BlockSpec((SSM_CONV, TN_IN), xbc_tile),
            pl.BlockSpec((1, TN_IN), xbc_tile),
        ],
        out_specs=[
            pl.BlockSpec((TM_IN, TN_IN), lambda i, j: (i, j)),
            pl.BlockSpec((TM_IN, LANES), lambda i, j: (i, 0)),
        ],
        out_shape=[
            jax.ShapeDtypeStruct((SEQ, D_PROJ), BF16),
            jax.ShapeDtypeStruct((SEQ, LANES), F32),
        ],
        scratch_shapes=[
            pltpu.VMEM((TM_IN, D_MODEL), BF16),
            pltpu.VMEM((D_XBC // TN_IN, HALO, TN_IN), F32),
        ],
        compiler_params=_params("arbitrary", "arbitrary"),
        name="in_proj",
    )(x, g, w, wdt, conv_w, conv_b)


def _gmlp_kernel(u_ref, v_ref, lng_ref, lnb_ref, ws_ref, bs_ref, ng_ref, ya_ref, wsm_scr, acc_scr):
    @pl.when(pl.program_id(0) == 0)
    def _():
        row = lax.broadcasted_iota(jnp.int32, (CHUNK, CHUNK), 0)
        col = lax.broadcasted_iota(jnp.int32, (CHUNK, CHUNK), 1)
        for g in range(N_GROUPS_A):
            wsm_scr[g] = jnp.where(row >= col, ws_ref[g], 0.0).astype(BF16)

    gi = lax.broadcasted_iota(jnp.int32, (2 * D_HEAD_A, 2 * D_HEAD_A), 0) // D_HEAD_A
    gj = lax.broadcasted_iota(jnp.int32, (2 * D_HEAD_A, 2 * D_HEAD_A), 1) // D_HEAD_A
    mean_m = jnp.where(gi == gj, 1.0 / D_HEAD_A, 0.0).astype(BF16)

    def group_mean(x_hi, x_lo=None):
        m = jnp.dot(x_hi, mean_m, preferred_element_type=F32)
        if x_lo is not None:
            m = m + jnp.dot(x_lo, mean_m, preferred_element_type=F32)
        return m

    def chunk_body(c, carry):
        r0 = pl.multiple_of(c * CHUNK, CHUNK)
        rows = pl.ds(r0, CHUNK)
        pairs = [slice(gp * 2 * D_HEAD_A, (gp + 1) * 2 * D_HEAD_A) for gp in range(N_GROUPS_A // 2)]
        vbs = [v_ref[rows, cols2] for cols2 in pairs]
        vcs = [vb.astype(F32) - group_mean(vb) for vb in vbs]
        sqs = [vc * vc for vc in vcs]
        sq_his = [sq.astype(BF16) for sq in sqs]
        vars_ = [group_mean(hi, (sq - hi.astype(F32)).astype(BF16)) for sq, hi in zip(sqs, sq_his)]
        vns = [(vc * lax.rsqrt(var + EPS) * lng_ref[:, cols2] + lnb_ref[:, cols2]).astype(BF16)
               for vc, var, cols2 in zip(vcs, vars_, pairs)]
        ssq = jnp.zeros((CHUNK, D_HEAD_A), F32)
        for g in range(N_GROUPS_A):
            cols = slice(g * D_HEAD_A, (g + 1) * D_HEAD_A)
            vn = vns[g // 2][:, (g % 2) * D_HEAD_A:(g % 2 + 1) * D_HEAD_A]
            sg = jnp.dot(wsm_scr[g], vn, preferred_element_type=F32) + bs_ref[g]
            ya = u_ref[rows, cols].astype(F32) * sg
            ssq = ssq + ya * ya
            acc_scr[rows, cols] = ya
        inv = lax.rsqrt(jnp.sum(ssq, axis=-1, keepdims=True) * (1.0 / D_A) + EPS)
        ya_ref[rows, :] = (acc_scr[rows, :] * inv * ng_ref[...]).astype(BF16)
        return carry

    lax.fori_loop(0, TM_GMLP // CHUNK, chunk_body, 0)


def _gmlp(proj, ln_g, ln_b, w_s, bs_full, norm_g):
    return pl.pallas_call(
        _gmlp_kernel,
        grid=(SEQ // TM_GMLP,),
        in_specs=[
            pl.BlockSpec((TM_GMLP, D_A), lambda i: (i, 0)),
            pl.BlockSpec((TM_GMLP, D_A), lambda i: (i, 1)),
            pl.BlockSpec((1, D_A), lambda i: (0, 0)),
            pl.BlockSpec((1, D_A), lambda i: (0, 0)),
            pl.BlockSpec((N_GROUPS_A, CHUNK, CHUNK), lambda i: (0, 0, 0)),
            pl.BlockSpec((N_GROUPS_A, CHUNK, D_HEAD_A), lambda i: (0, 0, 0)),
            pl.BlockSpec((1, D_A), lambda i: (0, 0)),
        ],
        out_specs=pl.BlockSpec((TM_GMLP, D_A), lambda i: (i, 0)),
        out_shape=jax.ShapeDtypeStruct((SEQ, D_A), BF16),
        scratch_shapes=[
            pltpu.VMEM((N_GROUPS_A, CHUNK, CHUNK), BF16),
            pltpu.VMEM((TM_GMLP, D_A), F32),
        ],
        compiler_params=_params("arbitrary"),
        name="gmlp",
    )(proj, proj, ln_g, ln_b, w_s, bs_full, norm_g)


def _ssd_kernel(zs_ref, xs_ref, bc_ref, dt_ref, dtb_ref, alog_ref, dskip_ref, ng_ref, yb_ref,
                h_scr, cs2_scr, srct_scr, w2t_scr):
    @pl.when(pl.program_id(0) == 0)
    def _():
        h_scr[...] = jnp.zeros((D_STATE, D_SSM), F32)

    row = lax.broadcasted_iota(jnp.int32, (CHUNK, CHUNK), 0)
    lane = lax.broadcasted_iota(jnp.int32, (CHUNK, CHUNK), 1)
    tril = row >= lane
    tri_b = jnp.where(tril, 1.0, 0.0).astype(BF16)
    left = lane < SSM_HEAD_DIM
    mask_l = jnp.where(left, 1.0, 0.0).astype(BF16)
    mask_r = jnp.where(left, 0.0, 1.0).astype(BF16)
    a_neg = -jnp.exp(alog_ref[...])

    dtv = dt_ref[...] + dtb_ref[...]
    dt_all = jnp.maximum(dtv, 0.0) + jnp.log1p(jnp.exp(-jnp.abs(dtv)))
    for c in range(R_SSD // CHUNK):
        crows = slice(c * CHUNK, (c + 1) * CHUNK)
        dt = dt_all[crows, :]
        adt = dt * a_neg
        hi = adt.astype(BF16)
        r1 = adt - hi.astype(F32)
        mid = r1.astype(BF16)
        lo = (r1 - mid.astype(F32)).astype(BF16)
        cs = (jnp.dot(tri_b, hi, preferred_element_type=F32)
              + jnp.dot(tri_b, mid, preferred_element_type=F32)
              + jnp.dot(tri_b, lo, preferred_element_type=F32))
        cs2 = cs * LOG2E
        cs2_scr[crows, :] = cs2
        srct_scr[crows, :] = (cs2 - jnp.log2(dt)).T
        w2t_scr[crows, :] = (dt * jnp.exp2(cs2[CHUNK - 1:CHUNK, :] - cs2)).T

    def chunk_body(c, carry):
        r0 = pl.multiple_of(c * CHUNK, CHUNK)
        rows = pl.ds(r0, CHUNK)
        cs2 = cs2_scr[rows, :]
        src_t = srct_scr[rows, :]
        w2_t = w2t_scr[rows, :]

        c_fs, bt_fs, scores_all = [], [], []
        for g in range(N_SSM_GROUPS):
            b_g = bc_ref[rows, g * D_STATE:(g + 1) * D_STATE]
            c_g = bc_ref[rows, N_SSM_GROUPS * D_STATE + g * D_STATE:
                         N_SSM_GROUPS * D_STATE + (g + 1) * D_STATE]
            bt_f = b_g.astype(F32).T
            c_fs.append(c_g.astype(F32))
            bt_fs.append(bt_f)
            scores_all.append(jnp.dot(c_g, bt_f.astype(BF16), preferred_element_type=F32))

        for g in range(N_SSM_GROUPS):
            c_f, bt_f, scores = c_fs[g], bt_fs[g], scores_all[g]
            ys = []
            ssq = jnp.zeros((CHUNK, 1), F32)
            for jp in range(2):
                pcols = slice((2 * g + jp) * LANES, (2 * g + jp + 1) * LANES)
                xp = xs_ref[rows, pcols]
                hp = h_scr[:, pcols]
                hpb = hp.astype(BF16)
                lhs_y, lhs_s, decay = [], [], []
                for hh in range(2):
                    h = g * HEADS_PER_GROUP + 2 * jp + hh
                    colv = jnp.sum(jnp.where(lane == h, cs2, 0.0), axis=-1, keepdims=True)
                    lmat = jnp.exp2(jnp.where(tril, colv - src_t[h:h + 1, :], -jnp.inf))
                    ecol = jnp.exp2(colv)
                    lhs_y += [(scores * lmat).astype(BF16), (c_f * ecol).astype(BF16)]
                    lhs_s.append((bt_f * w2_t[h:h + 1, :]).astype(BF16))
                    decay.append(jnp.exp2(colv[CHUNK - 1:CHUNK, :]))
                x_l, x_r = xp * mask_l, xp * mask_r
                yd = jnp.dot(jnp.concatenate(lhs_y, axis=1),
                             jnp.concatenate([x_l, hpb * mask_l, x_r, hpb * mask_r], axis=0),
                             preferred_element_type=F32)
                states = jnp.dot(jnp.concatenate(lhs_s, axis=1),
                                 jnp.concatenate([x_l, x_r], axis=0),
                                 preferred_element_type=F32)
                h_scr[:, pcols] = jnp.where(left, decay[0], decay[1]) * hp + states

                y = yd + dskip_ref[:, pcols] * xp.astype(F32)
                y = y * zs_ref[rows, pcols].astype(F32)
                ssq = ssq + jnp.sum(y * y, axis=-1, keepdims=True)
                ys.append((pcols, y))
            inv = lax.rsqrt(ssq * (1.0 / (2 * LANES)) + EPS)
            for pcols, y in ys:
                yb_ref[rows, pcols] = (y * inv * ng_ref[:, pcols]).astype(BF16)
        return carry

    lax.fori_loop(0, R_SSD // CHUNK, chunk_body, 0)


def _ssd(proj, dt_raw, dt_bias, a_log, d_skip_e, norm_g):
    blk = lambda col: pl.BlockSpec((R_SSD, D_SSM), lambda i: (i, col))
    full = lambda shape: pl.BlockSpec(shape, lambda i: (0,) * len(shape))
    return pl.pallas_call(
        _ssd_kernel,
        grid=(SEQ // R_SSD,),
        in_specs=[
            blk(2), blk(3), blk(4),
            pl.BlockSpec((R_SSD, LANES), lambda i: (i, 0)),
            full((1, LANES)), full((1, LANES)), full((1, D_SSM)), full((1, D_SSM)),
        ],
        out_specs=pl.BlockSpec((R_SSD, D_SSM), lambda i: (i, 0)),
        out_shape=jax.ShapeDtypeStruct((SEQ, D_SSM), BF16),
        scratch_shapes=[pltpu.VMEM((D_STATE, D_SSM), F32)] + [pltpu.VMEM((R_SSD, LANES), F32)] * 3,
        compiler_params=_params("arbitrary"),
        name="ssd",
    )(proj, proj, proj, dt_raw, dt_bias, a_log, d_skip_e, norm_g)


def _resident_proj_res_norm_kernel(n_lhs, tm, *refs):
    lhs_refs = refs[:n_lhs]
    w_ref, res_ref, g_ref, h_ref, n_ref = refs[n_lhs:]
    for r in range(tm // SUB_M):
        rows = slice(r * SUB_M, (r + 1) * SUB_M)
        h = res_ref[rows, :]
        k0 = 0
        for lhs in lhs_refs:
            k1 = k0 + lhs.shape[1]
            h = h + jnp.dot(lhs[rows, :], w_ref[k0:k1, :], preferred_element_type=F32)
            k0 = k1
        h_ref[rows, :] = h
        n_ref[rows, :] = (h * _rms_scale(h) * g_ref[...]).astype(BF16)


def _resident_proj_res_norm(lhs_list, w, res, g, tm, name):
    n_out = w.shape[1]
    in_specs = [pl.BlockSpec((tm, l.shape[1]), lambda i: (i, 0)) for l in lhs_list]
    in_specs += [
        pl.BlockSpec(w.shape, lambda i: (0, 0), pipeline_mode=pl.Buffered(1)),
        pl.BlockSpec((tm, n_out), lambda i: (i, 0)),
        pl.BlockSpec((1, n_out), lambda i: (0, 0)),
    ]
    return pl.pallas_call(
        functools.partial(_resident_proj_res_norm_kernel, len(lhs_list), tm),
        grid=(SEQ // tm,),
        in_specs=in_specs,
        out_specs=[
            pl.BlockSpec((tm, n_out), lambda i: (i, 0)),
            pl.BlockSpec((tm, n_out), lambda i: (i, 0)),
        ],
        out_shape=[
            jax.ShapeDtypeStruct((SEQ, n_out), F32),
            jax.ShapeDtypeStruct((SEQ, n_out), BF16),
        ],
        compiler_params=_params("arbitrary"),
        name=name,
    )(*lhs_list, w, res, g)


def _up_kernel(f_ref, wg_ref, wu_ref, cwg_ref, cwu_ref, cbg_ref, cbu_ref, act_ref, halo_scr):
    i = pl.program_id(0)
    j = pl.program_id(1)

    @pl.when(i == 0)
    def _():
        halo_scr[j] = jnp.zeros((HALO, 2 * TN_UP), F32)

    def conv(f, hist, w_ref, cw_ref, cb_ref):
        y = jnp.dot(f, w_ref[...], preferred_element_type=F32)
        full = jnp.concatenate([hist, y], axis=0)
        acc = cb_ref[...] + cw_ref[FFN_CONV - 1:FFN_CONV, :] * y
        for k in range(FFN_CONV - 1):
            shifted = pltpu.roll(full, FFN_CONV - 1 - k, axis=0)[HALO:, :]
            acc = acc + cw_ref[k:k + 1, :] * shifted
        return acc, y[y.shape[0] - HALO:, :]

    gcols, ucols = slice(0, TN_UP), slice(TN_UP, 2 * TN_UP)
    hist_g, hist_u = halo_scr[j, :, gcols], halo_scr[j, :, ucols]
    for r in range(TM_UP // SUB_UP):
        rows = slice(r * SUB_UP, (r + 1) * SUB_UP)
        f = f_ref[rows, :]
        gate, hist_g = conv(f, hist_g, wg_ref, cwg_ref, cbg_ref)
        up, hist_u = conv(f, hist_u, wu_ref, cwu_ref, cbu_ref)
        act_ref[rows, :] = (gate * _sigmoid(gate) * up).astype(BF16)
    halo_scr[j, :, gcols] = hist_g
    halo_scr[j, :, ucols] = hist_u


def _up_proj(f, w_up, conv_w, conv_b):
    n_t = D_FF // TN_UP
    return pl.pallas_call(
        _up_kernel,
        grid=(SEQ // TM_UP, n_t),
        in_specs=[
            pl.BlockSpec((TM_UP, D_MODEL), lambda i, j: (i, 0)),
            pl.BlockSpec((D_MODEL, TN_UP), lambda i, j: (0, j)),
            pl.BlockSpec((D_MODEL, TN_UP), lambda i, j: (0, j + n_t)),
            pl.BlockSpec((FFN_CONV, TN_UP), lambda i, j: (0, j)),
            pl.BlockSpec((FFN_CONV, TN_UP), lambda i, j: (0, j + n_t)),
            pl.BlockSpec((1, TN_UP), lambda i, j: (0, j)),
            pl.BlockSpec((1, TN_UP), lambda i, j: (0, j + n_t)),
        ],
        out_specs=pl.BlockSpec((TM_UP, TN_UP), lambda i, j: (i, j)),
        out_shape=jax.ShapeDtypeStruct((SEQ, D_FF), BF16),
        scratch_shapes=[pltpu.VMEM((n_t, HALO, 2 * TN_UP), F32)],
        compiler_params=_params("arbitrary", "arbitrary"),
        name="up_proj",
    )(f, w_up, w_up, conv_w, conv_w, conv_b, conv_b)


def _down_ple_kernel(act_ref, wd_ref, h1_ref, gp_ref, wpg_ref, p_ref, wple_ref, gf_ref, o_ref):
    emb = jnp.dot(p_ref[...].astype(BF16), wple_ref[...], preferred_element_type=F32)
    h2 = h1_ref[...] + jnp.dot(act_ref[...], wd_ref[...], preferred_element_type=F32)
    n3 = (h2 * _rms_scale(h2) * gp_ref[...]).astype(BF16)
    gate = _sigmoid(jnp.dot(n3, wpg_ref[...], preferred_element_type=F32))
    h3 = h2 + gate * emb
    o_ref[...] = h3 * _rms_scale(h3) * gf_ref[...]


def _down_ple(act, w_down, h1, g_ple, w_pg, p, w_ple, g_final):
    rows = lambda width: pl.BlockSpec((TM_DOWN, width), lambda i: (i, 0))
    whole = lambda a: pl.BlockSpec(a.shape, lambda i: (0, 0), pipeline_mode=pl.Buffered(1))
    vec = pl.BlockSpec((1, D_MODEL), lambda i: (0, 0))
    return pl.pallas_call(
        _down_ple_kernel,
        grid=(SEQ // TM_DOWN,),
        in_specs=[rows(D_FF), whole(w_down), rows(D_MODEL), vec, whole(w_pg), rows(D_PLE),
                  whole(w_ple), vec],
        out_specs=rows(D_MODEL),
        out_shape=jax.ShapeDtypeStruct((SEQ, D_MODEL), F32),
        compiler_params=_params("arbitrary"),
        name="down_ple",
    )(act, w_down, h1, g_ple, w_pg, p, w_ple, g_final)


def _layer(h, p_i, norm_mix_g, w_in, ln_a_g, ln_a_b, w_s, b_s, norm_a_g, conv_ssm_w, conv_ssm_b,
           dt_bias, a_log, d_skip, ssm_norm_g, w_out, norm_ffn_g, w_up, conv_ffn_w, conv_ffn_b,
           w_down, norm_ple_g, w_ple_gate, w_ple, norm_final_g):
    row = lambda v: v.reshape(1, -1)
    pad_lanes = lambda v: jnp.pad(v, (0, LANES - v.shape[0])).reshape(1, LANES)

    w_dt = jnp.pad(w_in[:, D_PROJ:], ((0, 0), (0, LANES - N_SSM_HEADS))).astype(BF16)
    proj, dt_raw = _in_proj(h, row(norm_mix_g), w_in.astype(BF16), w_dt, conv_ssm_w, row(conv_ssm_b))

    bs_full = jnp.broadcast_to(b_s[:, :, None], (N_GROUPS_A, CHUNK, D_HEAD_A))
    y_a = _gmlp(proj, row(ln_a_g), row(ln_a_b), w_s, bs_full, row(norm_a_g))

    y_b = _ssd(proj, dt_raw, pad_lanes(dt_bias), pad_lanes(a_log),
               row(jnp.repeat(d_skip, SSM_HEAD_DIM)), row(ssm_norm_g))

    h1, f = _resident_proj_res_norm([y_a, y_b], w_out.astype(BF16), h, row(norm_ffn_g),
                                    TM_OUT, "out_proj")
    act = _up_proj(f, w_up.astype(BF16), conv_ffn_w, row(conv_ffn_b))
    return _down_ple(act, w_down.astype(BF16), h1, row(norm_ple_g), w_ple_gate.astype(BF16), p_i,
                     w_ple.astype(BF16), row(norm_final_g))


def kernel(x, p, norm_mix_g, w_in, ln_a_g, ln_a_b, w_s, b_s, norm_a_g, conv_ssm_w, conv_ssm_b,
           dt_bias, a_log, d_skip, ssm_norm_g, w_out, norm_ffn_g, w_up, conv_ffn_w, conv_ffn_b,
           w_down, norm_ple_g, w_ple_gate, w_ple, norm_final_g):
    assert x.shape == (1, SEQ, D_MODEL) and w_in.shape[0] == 1
    out = _layer(
        x[0], p[0, 0], norm_mix_g[0], w_in[0], ln_a_g[0], ln_a_b[0], w_s[0], b_s[0], norm_a_g[0],
        conv_ssm_w[0], conv_ssm_b[0], dt_bias[0], a_log[0], d_skip[0], ssm_norm_g[0], w_out[0],
        norm_ffn_g[0], w_up[0], conv_ffn_w[0], conv_ffn_b[0], w_down[0], norm_ple_g[0],
        w_ple_gate[0], w_ple[0], norm_final_g)
    return out[None]
```

```python
import functools
import math

import jax
import jax.numpy as jnp
from jax import lax
from jax.experimental import pallas as pl
from jax.experimental.pallas import tpu as pltpu

F32 = jnp.float32
BF16 = jnp.bfloat16

D_MODEL = 2048
SEQ = 8192
D_A = 2048
CHUNK = 128
N_GROUPS_A = 16
D_HEAD_A = 128
D_SSM = 2048
SSM_HEAD_DIM = 64
N_SSM_HEADS = 32
N_SSM_GROUPS = 8
HEADS_PER_GROUP = 4
D_STATE = 128
SSM_CONV = 4
D_XBC = 4096
D_PROJ = 10240
D_FF = 5632
FFN_CONV = 3
D_PLE = 256
EPS = 1e-6
LOG2E = 1.0 / math.log(2.0)

LANES = 128
SUBLANES = 8
HALO = SUBLANES
VMEM_LIMIT_BYTES = 56 * 1024 * 1024

TM_IN = 1024
TN_IN = 1024
N_UV_TILES = (2 * D_A) // TN_IN
N_Z_END = (2 * D_A + D_SSM) // TN_IN
SUB_M = 256
TM_GMLP = 512
R_SSD = 512
TM_OUT = 512
TM_UP = 1024
SUB_UP = 1024
TN_UP = 512
TM_DOWN = 256


def _params(*sem):
    return pltpu.CompilerParams(dimension_semantics=sem, vmem_limit_bytes=VMEM_LIMIT_BYTES)


def _sigmoid(x):
    return 1.0 / (1.0 + jnp.exp(-x))


def _rms_scale(x):
    return lax.rsqrt(jnp.mean(x * x, axis=-1, keepdims=True) + EPS)


def _dot_nt(a, bt):
    return lax.dot_general(a, bt, (((1,), (1,)), ((), ())), preferred_element_type=F32)


def _in_proj_kernel(x_ref, g_ref, wt_ref, wdt_ref, cw_ref, cb_ref, proj_ref, dt_ref,
                    a_scr, halo_scr):
    i = pl.program_id(0)
    j = pl.program_id(1)
    t = jnp.maximum(j - N_Z_END, 0)

    @pl.when(j == 0)
    def _():
        x = x_ref[...]
        a = (x * _rms_scale(x) * g_ref[...]).astype(BF16)
        a_scr[...] = a
        dt_ref[...] = _dot_nt(a, wdt_ref[...])

    @pl.when((i == 0) & (j >= N_Z_END))
    def _():
        halo_scr[t] = jnp.zeros((HALO, TN_IN), F32)

    @pl.when(j < N_UV_TILES)
    def _():
        y = _dot_nt(a_scr[...], wt_ref[...].astype(BF16))
        proj_ref[...] = jax.nn.gelu(y).astype(BF16)

    @pl.when((j >= N_UV_TILES) & (j < N_Z_END))
    def _():
        y = _dot_nt(a_scr[...], wt_ref[...].astype(BF16))
        proj_ref[...] = (y * _sigmoid(y)).astype(BF16)

    @pl.when(j >= N_Z_END)
    def _():
        y = _dot_nt(a_scr[...], wt_ref[...].astype(BF16))
        full = jnp.concatenate([halo_scr[t], y], axis=0)
        halo_scr[t] = y[TM_IN - HALO:, :]
        acc = cb_ref[...] + cw_ref[SSM_CONV - 1:SSM_CONV, :] * y
        for k in range(SSM_CONV - 1):
            shifted = pltpu.roll(full, SSM_CONV - 1 - k, axis=0)[HALO:, :]
            acc = acc + cw_ref[k:k + 1, :] * shifted
        proj_ref[...] = (acc * _sigmoid(acc)).astype(BF16)


def _in_proj(x, g, w, wdt, conv_w, conv_b):
    xbc_tile = lambda i, j: (0, jnp.maximum(j - N_Z_END, 0))
    return pl.pallas_call(
        _in_proj_kernel,
        grid=(SEQ // TM_IN, D_PROJ // TN_IN),
        in_specs=[
            pl.BlockSpec((TM_IN, D_MODEL), lambda i, j: (i, 0)),
            pl.BlockSpec((1, D_MODEL), lambda i, j: (0, 0)),
            pl.BlockSpec((TN_IN, D_MODEL), lambda i, j: (j, 0)),
            pl.BlockSpec((LANES, D_MODEL), lambda i, j: (0, 0)),
            pl.BlockSpec((SSM_CONV, TN_IN), xbc_tile),
            pl.BlockSpec((1, TN_IN), xbc_tile),
        ],
        out_specs=[
            pl.BlockSpec((TM_IN, TN_IN), lambda i, j: (i, j)),
            pl.BlockSpec((TM_IN, LANES), lambda i, j: (i, 0)),
        ],
        out_shape=[
            jax.ShapeDtypeStruct((SEQ, D_PROJ), BF16),
            jax.ShapeDtypeStruct((SEQ, LANES), F32),
        ],
        scratch_shapes=[
            pltpu.VMEM((TM_IN, D_MODEL), BF16),
            pltpu.VMEM((D_XBC // TN_IN, HALO, TN_IN), F32),
        ],
        compiler_params=_params("arbitrary", "arbitrary"),
        name="in_proj",
    )(x, g, w, wdt, conv_w, conv_b)


def _gmlp_kernel(u_ref, v_ref, lng_ref, lnb_ref, ws_ref, bs_ref, ng_ref, ya_ref, wsm_scr, acc_scr):
    @pl.when(pl.program_id(0) == 0)
    def _():
        row = lax.broadcasted_iota(jnp.int32, (CHUNK, CHUNK), 0)
        col = lax.broadcasted_iota(jnp.int32, (CHUNK, CHUNK), 1)
        for g in range(N_GROUPS_A):
            wsm_scr[g] = jnp.where(row >= col, ws_ref[g], 0.0).astype(BF16)

    gi = lax.broadcasted_iota(jnp.int32, (2 * D_HEAD_A, 2 * D_HEAD_A), 0) // D_HEAD_A
    gj = lax.broadcasted_iota(jnp.int32, (2 * D_HEAD_A, 2 * D_HEAD_A), 1) // D_HEAD_A
    mean_m = jnp.where(gi == gj, 1.0 / D_HEAD_A, 0.0).astype(BF16)

    def group_mean(x_hi, x_lo=None):
        m = jnp.dot(x_hi, mean_m, preferred_element_type=F32)
        if x_lo is not None:
            m = m + jnp.dot(x_lo, mean_m, preferred_element_type=F32)
        return m

    def chunk_body(c, carry):
        r0 = pl.multiple_of(c * CHUNK, CHUNK)
        rows = pl.ds(r0, CHUNK)
        pairs = [slice(gp * 2 * D_HEAD_A, (gp + 1) * 2 * D_HEAD_A) for gp in range(N_GROUPS_A // 2)]
        vbs = [v_ref[rows, cols2] for cols2 in pairs]
        vcs = [vb.astype(F32) - group_mean(vb) for vb in vbs]
        sqs = [vc * vc for vc in vcs]
        sq_his = [sq.astype(BF16) for sq in sqs]
        vars_ = [group_mean(hi, (sq - hi.astype(F32)).astype(BF16)) for sq, hi in zip(sqs, sq_his)]
        vns = [(vc * lax.rsqrt(var + EPS) * lng_ref[:, cols2] + lnb_ref[:, cols2]).astype(BF16)
               for vc, var, cols2 in zip(vcs, vars_, pairs)]
        ssq = jnp.zeros((CHUNK, D_HEAD_A), F32)
        for g in range(N_GROUPS_A):
            cols = slice(g * D_HEAD_A, (g + 1) * D_HEAD_A)
            vn = vns[g // 2][:, (g % 2) * D_HEAD_A:(g % 2 + 1) * D_HEAD_A]
            sg = jnp.dot(wsm_scr[g], vn, preferred_element_type=F32) + bs_ref[g]
            ya = u_ref[rows, cols].astype(F32) * sg
            ssq = ssq + ya * ya
            acc_scr[rows, cols] = ya
        inv = lax.rsqrt(jnp.sum(ssq, axis=-1, keepdims=True) * (1.0 / D_A) + EPS)
        ya_ref[rows, :] = (acc_scr[rows, :] * inv * ng_ref[...]).astype(BF16)
        return carry

    lax.fori_loop(0, TM_GMLP // CHUNK, chunk_body, 0)


def _gmlp(proj, ln_g, ln_b, w_s, bs_full, norm_g):
    return pl.pallas_call(
        _gmlp_kernel,
        grid=(SEQ // TM_GMLP,),
        in_specs=[
            pl.BlockSpec((TM_GMLP, D_A), lambda i: (i, 0)),
            pl.BlockSpec((TM_GMLP, D_A), lambda i: (i, 1)),
            pl.BlockSpec((1, D_A), lambda i: (0, 0)),
            pl.BlockSpec((1, D_A), lambda i: (0, 0)),
            pl.BlockSpec((N_GROUPS_A, CHUNK, CHUNK), lambda i: (0, 0, 0)),
            pl.BlockSpec((N_GROUPS_A, CHUNK, D_HEAD_A), lambda i: (0, 0, 0)),
            pl.BlockSpec((1, D_A), lambda i: (0, 0)),
        ],
        out_specs=pl.BlockSpec((TM_GMLP, D_A), lambda i: (i, 0)),
        out_shape=jax.ShapeDtypeStruct((SEQ, D_A), BF16),
        scratch_shapes=[
            pltpu.VMEM((N_GROUPS_A, CHUNK, CHUNK), BF16),
            pltpu.VMEM((TM_GMLP, D_A), F32),
        ],
        compiler_params=_params("arbitrary"),
        name="gmlp",
    )(proj, proj, ln_g, ln_b, w_s, bs_full, norm_g)


def _ssd_kernel(zs_ref, xs_ref, bc_ref, dt_ref, dtb_ref, alog_ref, dskip_ref, ng_ref, yb_ref,
                h_scr, cs2_scr, srct_scr, w2t_scr):
    @pl.when(pl.program_id(0) == 0)
    def _():
        h_scr[...] = jnp.zeros((D_STATE, D_SSM), F32)

    row = lax.broadcasted_iota(jnp.int32, (CHUNK, CHUNK), 0)
    lane = lax.broadcasted_iota(jnp.int32, (CHUNK, CHUNK), 1)
    tril = row >= lane
    tri_b = jnp.where(tril, 1.0, 0.0).astype(BF16)
    left = lane < SSM_HEAD_DIM
    mask_l = jnp.where(left, 1.0, 0.0).astype(BF16)
    mask_r = jnp.where(left, 0.0, 1.0).astype(BF16)
    a_neg = -jnp.exp(alog_ref[...])

    dtv = dt_ref[...] + dtb_ref[...]
    dt_all = jnp.maximum(dtv, 0.0) + jnp.log1p(jnp.exp(-jnp.abs(dtv)))
    for c in range(R_SSD // CHUNK):
        crows = slice(c * CHUNK, (c + 1) * CHUNK)
        dt = dt_all[crows, :]
        adt = dt * a_neg
        hi = adt.astype(BF16)
        r1 = adt - hi.astype(F32)
        mid = r1.astype(BF16)
        lo = (r1 - mid.astype(F32)).astype(BF16)
        cs = (jnp.dot(tri_b, hi, preferred_element_type=F32)
              + jnp.dot(tri_b, mid, preferred_element_type=F32)
              + jnp.dot(tri_b, lo, preferred_element_type=F32))
        cs2 = cs * LOG2E
        cs2_scr[crows, :] = cs2
        srct_scr[crows, :] = (cs2 - jnp.log2(dt)).T
        w2t_scr[crows, :] = (dt * jnp.exp2(cs2[CHUNK - 1:CHUNK, :] - cs2)).T

    def chunk_body(c, carry):
        r0 = pl.multiple_of(c * CHUNK, CHUNK)
        rows = pl.ds(r0, CHUNK)
        cs2 = cs2_scr[rows, :]
        src_t = srct_scr[rows, :]
        w2_t = w2t_scr[rows, :]

        c_fs, bt_fs, scores_all = [], [], []
        for g in range(N_SSM_GROUPS):
            b_g = bc_ref[rows, g * D_STATE:(g + 1) * D_STATE]
            c_g = bc_ref[rows, N_SSM_GROUPS * D_STATE + g * D_STATE:
                         N_SSM_GROUPS * D_STATE + (g + 1) * D_STATE]
            bt_f = b_g.astype(F32).T
            c_fs.append(c_g.astype(F32))
            bt_fs.append(bt_f)
            scores_all.append(jnp.dot(c_g, bt_f.astype(BF16), preferred_element_type=F32))

        for g in range(N_SSM_GROUPS):
            c_f, bt_f, scores = c_fs[g], bt_fs[g], scores_all[g]
            ys = []
            ssq = jnp.zeros((CHUNK, 1), F32)
            for jp in range(2):
                pcols = slice((2 * g + jp) * LANES, (2 * g + jp + 1) * LANES)
                xp = xs_ref[rows, pcols]
                hp = h_scr[:, pcols]
                hpb = hp.astype(BF16)
                lhs_y, lhs_s, decay = [], [], []
                for hh in range(2):
                    h = g * HEADS_PER_GROUP + 2 * jp + hh
                    colv = jnp.sum(jnp.where(lane == h, cs2, 0.0), axis=-1, keepdims=True)
                    lmat = jnp.exp2(jnp.where(tril, colv - src_t[h:h + 1, :], -jnp.inf))
                    ecol = jnp.exp2(colv)
                    lhs_y += [(scores * lmat).astype(BF16), (c_f * ecol).astype(BF16)]
                    lhs_s.append((bt_f * w2_t[h:h + 1, :]).astype(BF16))
                    decay.append(jnp.exp2(colv[CHUNK - 1:CHUNK, :]))
                x_l, x_r = xp * mask_l, xp * mask_r
                yd = jnp.dot(jnp.concatenate(lhs_y, axis=1),
                             jnp.concatenate([x_l, hpb * mask_l, x_r, hpb * mask_r], axis=0),
                             preferred_element_type=F32)
                states = jnp.dot(jnp.concatenate(lhs_s, axis=1),
                                 jnp.concatenate([x_l, x_r], axis=0),
                                 preferred_element_type=F32)
                h_scr[:, pcols] = jnp.where(left, decay[0], decay[1]) * hp + states

                y = yd + dskip_ref[:, pcols] * xp.astype(F32)
                y = y * zs_ref[rows, pcols].astype(F32)
                ssq = ssq + jnp.sum(y * y, axis=-1, keepdims=True)
                ys.append((pcols, y))
            inv = lax.rsqrt(ssq * (1.0 / (2 * LANES)) + EPS)
            for pcols, y in ys:
                yb_ref[rows, pcols] = (y * inv * ng_ref[:, pcols]).astype(BF16)
        return carry

    lax.fori_loop(0, R_SSD // CHUNK, chunk_body, 0)


def _ssd(proj, dt_raw, dt_bias, a_log, d_skip_e, norm_g):
    blk = lambda col: pl.BlockSpec((R_SSD, D_SSM), lambda i: (i, col))
    full = lambda shape: pl.BlockSpec(shape, lambda i: (0,) * len(shape))
    return pl.pallas_call(
        _ssd_kernel,
        grid=(SEQ // R_SSD,),
        in_specs=[
            blk(2), blk(3), blk(4),
            pl.BlockSpec((R_SSD, LANES), lambda i: (i, 0)),
            full((1, LANES)), full((1, LANES)), full((1, D_SSM)), full((1, D_SSM)),
        ],
        out_specs=pl.BlockSpec((R_SSD, D_SSM), lambda i: (i, 0)),
        out_shape=jax.ShapeDtypeStruct((SEQ, D_SSM), BF16),
        scratch_shapes=[pltpu.VMEM((D_STATE, D_SSM), F32)] + [pltpu.VMEM((R_SSD, LANES), F32)] * 3,
        compiler_params=_params("arbitrary"),
        name="ssd",
    )(proj, proj, proj, dt_raw, dt_bias, a_log, d_skip_e, norm_g)


def _resident_proj_res_norm_kernel(n_lhs, tm, *refs):
    lhs_refs = refs[:n_lhs]
    w_ref, res_ref, g_ref, h_ref, n_ref = refs[n_lhs:]
    for r in range(tm // SUB_M):
        rows = slice(r * SUB_M, (r + 1) * SUB_M)
        h = res_ref[rows, :]
        k0 = 0
        for lhs in lhs_refs:
            k1 = k0 + lhs.shape[1]
            h = h + jnp.dot(lhs[rows, :], w_ref[k0:k1, :], preferred_element_type=F32)
            k0 = k1
        h_ref[rows, :] = h
        n_ref[rows, :] = (h * _rms_scale(h) * g_ref[...]).astype(BF16)


def _resident_proj_res_norm(lhs_list, w, res, g, tm, name):
    n_out = w.shape[1]
    in_specs = [pl.BlockSpec((tm, l.shape[1]), lambda i: (i, 0)) for l in lhs_list]
    in_specs += [
        pl.BlockSpec(w.shape, lambda i: (0, 0), pipeline_mode=pl.Buffered(1)),
        pl.BlockSpec((tm, n_out), lambda i: (i, 0)),
        pl.BlockSpec((1, n_out), lambda i: (0, 0)),
    ]
    return pl.pallas_call(
        functools.partial(_resident_proj_res_norm_kernel, len(lhs_list), tm),
        grid=(SEQ // tm,),
        in_specs=in_specs,
        out_specs=[
            pl.BlockSpec((tm, n_out), lambda i: (i, 0)),
            pl.BlockSpec((tm, n_out), lambda i: (i, 0)),
        ],
        out_shape=[
            jax.ShapeDtypeStruct((SEQ, n_out), F32),
            jax.ShapeDtypeStruct((SEQ, n_out), BF16),
        ],
        compiler_params=_params("arbitrary"),
        name=name,
    )(*lhs_list, w, res, g)


def _up_kernel(f_ref, wg_ref, wu_ref, cwg_ref, cwu_ref, cbg_ref, cbu_ref, act_ref, halo_scr):
    i = pl.program_id(0)
    j = pl.program_id(1)

    @pl.when(i == 0)
    def _():
        halo_scr[j] = jnp.zeros((HALO, 2 * TN_UP), F32)

    def conv(f, hist, w_ref, cw_ref, cb_ref):
        y = jnp.dot(f, w_ref[...], preferred_element_type=F32)
        full = jnp.concatenate([hist, y], axis=0)
        acc = cb_ref[...] + cw_ref[FFN_CONV - 1:FFN_CONV, :] * y
        for k in range(FFN_CONV - 1):
            shifted = pltpu.roll(full, FFN_CONV - 1 - k, axis=0)[HALO:, :]
            acc = acc + cw_ref[k:k + 1, :] * shifted
        return acc, y[y.shape[0] - HALO:, :]

    gcols, ucols = slice(0, TN_UP), slice(TN_UP, 2 * TN_UP)
    hist_g, hist_u = halo_scr[j, :, gcols], halo_scr[j, :, ucols]
    for r in range(TM_UP // SUB_UP):
        rows = slice(r * SUB_UP, (r + 1) * SUB_UP)
        f = f_ref[rows, :]
        gate, hist_g = conv(f, hist_g, wg_ref, cwg_ref, cbg_ref)
        up, hist_u = conv(f, hist_u, wu_ref, cwu_ref, cbu_ref)
        act_ref[rows, :] = (gate * _sigmoid(gate) * up).astype(BF16)
    halo_scr[j, :, gcols] = hist_g
    halo_scr[j, :, ucols] = hist_u


def _up_proj(f, w_up, conv_w, conv_b):
    n_t = D_FF // TN_UP
    return pl.pallas_call(
        _up_kernel,
        grid=(SEQ // TM_UP, n_t),
        in_specs=[
            pl.BlockSpec((TM_UP, D_MODEL), lambda i, j: (i, 0)),
            pl.BlockSpec((D_MODEL, TN_UP), lambda i, j: (0, j)),
            pl.BlockSpec((D_MODEL, TN_UP), lambda i, j: (0, j + n_t)),
            pl.BlockSpec((FFN_CONV, TN_UP), lambda i, j: (0, j)),
            pl.BlockSpec((FFN_CONV, TN_UP), lambda i, j: (0, j + n_t)),
            pl.BlockSpec((1, TN_UP), lambda i, j: (0, j)),
            pl.BlockSpec((1, TN_UP), lambda i, j: (0, j + n_t)),
        ],
        out_specs=pl.BlockSpec((TM_UP, TN_UP), lambda i, j: (i, j)),
        out_shape=jax.ShapeDtypeStruct((SEQ, D_FF), BF16),
        scratch_shapes=[pltpu.VMEM((n_t, HALO, 2 * TN_UP), F32)],
        compiler_params=_params("arbitrary", "arbitrary"),
        name="up_proj",
    )(f, w_up, w_up, conv_w, conv_w, conv_b, conv_b)


def _down_ple_kernel(act_ref, wd_ref, h1_ref, gp_ref, wpg_ref, p_ref, wple_ref, gf_ref, o_ref):
    emb = jnp.dot(p_ref[...].astype(BF16), wple_ref[...], preferred_element_type=F32)
    h2 = h1_ref[...] + jnp.dot(act_ref[...], wd_ref[...], preferred_element_type=F32)
    n3 = (h2 * _rms_scale(h2) * gp_ref[...]).astype(BF16)
    gate = _sigmoid(jnp.dot(n3, wpg_ref[...], preferred_element_type=F32))
    h3 = h2 + gate * emb
    o_ref[...] = h3 * _rms_scale(h3) * gf_ref[...]


def _down_ple(act, w_down, h1, g_ple, w_pg, p, w_ple, g_final):
    rows = lambda width: pl.BlockSpec((TM_DOWN, width), lambda i: (i, 0))
    whole = lambda a: pl.BlockSpec(a.shape, lambda i: (0, 0), pipeline_mode=pl.Buffered(1))
    vec = pl.BlockSpec((1, D_MODEL), lambda i: (0, 0))
    return pl.pallas_call(
        _down_ple_kernel,
        grid=(SEQ // TM_DOWN,),
        in_specs=[rows(D_FF), whole(w_down), rows(D_MODEL), vec, whole(w_pg), rows(D_PLE),
                  whole(w_ple), vec],
        out_specs=rows(D_MODEL),
        out_shape=jax.ShapeDtypeStruct((SEQ, D_MODEL), F32),
        compiler_params=_params("arbitrary"),
        name="down_ple",
    )(act, w_down, h1, g_ple, w_pg, p, w_ple, g_final)


def _layer(h, p_i, norm_mix_g, w_in, ln_a_g, ln_a_b, w_s, b_s, norm_a_g, conv_ssm_w, conv_ssm_b,
           dt_bias, a_log, d_skip, ssm_norm_g, w_out, norm_ffn_g, w_up, conv_ffn_w, conv_ffn_b,
           w_down, norm_ple_g, w_ple_gate, w_ple, norm_final_g):
    row = lambda v: v.reshape(1, -1)
    pad_lanes = lambda v: jnp.pad(v, (0, LANES - v.shape[0])).reshape(1, LANES)

    w_in_t = jnp.swapaxes(w_in, 0, 1)
    w_dt_t = jnp.pad(w_in_t[D_PROJ:], ((0, LANES - N_SSM_HEADS), (0, 0))).astype(BF16)
    proj, dt_raw = _in_proj(h, row(norm_mix_g), w_in_t, w_dt_t, conv_ssm_w, row(conv_ssm_b))

    bs_full = jnp.broadcast_to(b_s[:, :, None], (N_GROUPS_A, CHUNK, D_HEAD_A))
    y_a = _gmlp(proj, row(ln_a_g), row(ln_a_b), w_s, bs_full, row(norm_a_g))

    y_b = _ssd(proj, dt_raw, pad_lanes(dt_bias), pad_lanes(a_log),
               row(jnp.repeat(d_skip, SSM_HEAD_DIM)), row(ssm_norm_g))

    h1, f = _resident_proj_res_norm([y_a, y_b], w_out.astype(BF16), h, row(norm_ffn_g),
                                    TM_OUT, "out_proj")
    act = _up_proj(f, w_up.astype(BF16), conv_ffn_w, row(conv_ffn_b))
    return _down_ple(act, w_down.astype(BF16), h1, row(norm_ple_g), w_ple_gate.astype(BF16), p_i,
                     w_ple.astype(BF16), row(norm_final_g))


def kernel(x, p, norm_mix_g, w_in, ln_a_g, ln_a_b, w_s, b_s, norm_a_g, conv_ssm_w, conv_ssm_b,
           dt_bias, a_log, d_skip, ssm_norm_g, w_out, norm_ffn_g, w_up, conv_ffn_w, conv_ffn_b,
           w_down, norm_ple_g, w_ple_gate, w_ple, norm_final_g):
    assert x.shape == (1, SEQ, D_MODEL) and w_in.shape[0] == 1
    out = _layer(
        x[0], p[0, 0], norm_mix_g[0], w_in[0], ln_a_g[0], ln_a_b[0], w_s[0], b_s[0], norm_a_g[0],
        conv_ssm_w[0], conv_ssm_b[0], dt_bias[0], a_log[0], d_skip[0], ssm_norm_g[0], w_out[0],
        norm_ffn_g[0], w_up[0], conv_ffn_w[0], conv_ffn_b[0], w_down[0], norm_ple_g[0],
        w_ple_gate[0], w_ple[0], norm_final_g)
    return out[None]
```

```python
import functools
import math

import jax
import jax.numpy as jnp
from jax import lax
from jax.experimental import pallas as pl
from jax.experimental.pallas import tpu as pltpu

F32 = jnp.float32
BF16 = jnp.bfloat16

D_MODEL = 2048
SEQ = 8192
D_A = 2048
CHUNK = 128
N_GROUPS_A = 16
D_HEAD_A = 128
D_SSM = 2048
SSM_HEAD_DIM = 64
N_SSM_HEADS = 32
N_SSM_GROUPS = 8
HEADS_PER_GROUP = 4
D_STATE = 128
SSM_CONV = 4
D_XBC = 4096
D_PROJ = 10240
D_FF = 5632
FFN_CONV = 3
D_PLE = 256
EPS = 1e-6
LOG2E = 1.0 / math.log(2.0)

LANES = 128
SUBLANES = 8
HALO = SUBLANES
VMEM_LIMIT_BYTES = 56 * 1024 * 1024

TM_IN = 1024
TN_IN = 1024
N_UV_TILES = (2 * D_A) // TN_IN
N_Z_END = (2 * D_A + D_SSM) // TN_IN
SUB_M = 256
TM_GMLP = 512
R_SSD = 512
TM_OUT = 512
TM_UP = 1024
SUB_UP = 1024
TN_UP = 512
TM_DOWN = 256


def _params(*sem):
    return pltpu.CompilerParams(dimension_semantics=sem, vmem_limit_bytes=VMEM_LIMIT_BYTES)


def _sigmoid(x):
    return 1.0 / (1.0 + jnp.exp(-x))


def _rms_scale(x):
    return lax.rsqrt(jnp.mean(x * x, axis=-1, keepdims=True) + EPS)


def _dot_nt(a, bt):
    return lax.dot_general(a, bt, (((1,), (1,)), ((), ())), preferred_element_type=F32)


def _in_proj_kernel(x_ref, g_ref, wt_ref, wdt_ref, cw_ref, cb_ref, proj_ref, dt_ref,
                    a_scr, halo_scr):
    i = pl.program_id(0)
    j = pl.program_id(1)
    t = jnp.maximum(j - N_Z_END, 0)

    @pl.when(j == 0)
    def _():
        x = x_ref[...]
        a = (x * _rms_scale(x) * g_ref[...]).astype(BF16)
        a_scr[...] = a
        dt_ref[...] = _dot_nt(a, wdt_ref[...])

    @pl.when((i == 0) & (j >= N_Z_END))
    def _():
        halo_scr[t] = jnp.zeros((HALO, TN_IN), F32)

    @pl.when(j < N_UV_TILES)
    def _():
        y = _dot_nt(a_scr[...], wt_ref[...].astype(BF16))
        proj_ref[...] = jax.nn.gelu(y).astype(BF16)

    @pl.when((j >= N_UV_TILES) & (j < N_Z_END))
    def _():
        y = _dot_nt(a_scr[...], wt_ref[...].astype(BF16))
        proj_ref[...] = (y * _sigmoid(y)).astype(BF16)

    @pl.when(j >= N_Z_END)
    def _():
        y = _dot_nt(a_scr[...], wt_ref[...].astype(BF16))
        full = jnp.concatenate([halo_scr[t], y], axis=0)
        halo_scr[t] = y[TM_IN - HALO:, :]
        acc = cb_ref[...] + cw_ref[SSM_CONV - 1:SSM_CONV, :] * y
        for k in range(SSM_CONV - 1):
            shifted = pltpu.roll(full, SSM_CONV - 1 - k, axis=0)[HALO:, :]
            acc = acc + cw_ref[k:k + 1, :] * shifted
        proj_ref[...] = (acc * _sigmoid(acc)).astype(BF16)


def _in_proj(x, g, w, wdt, conv_w, conv_b):
    xbc_tile = lambda i, j: (0, jnp.maximum(j - N_Z_END, 0))
    return pl.pallas_call(
        _in_proj_kernel,
        grid=(SEQ // TM_IN, D_PROJ // TN_IN),
        in_specs=[
            pl.BlockSpec((TM_IN, D_MODEL), lambda i, j: (i, 0)),
            pl.BlockSpec((1, D_MODEL), lambda i, j: (0, 0)),
            pl.BlockSpec((TN_IN, D_MODEL), lambda i, j: (j, 0)),
            pl.BlockSpec((LANES, D_MODEL), lambda i, j: (0, 0)),
            pl.BlockSpec((SSM_CONV, TN_IN), xbc_tile),
            pl.BlockSpec((1, TN_IN), xbc_tile),
        ],
        out_specs=[
            pl.BlockSpec((TM_IN, TN_IN), lambda i, j: (i, j)),
            pl.BlockSpec((TM_IN, LANES), lambda i, j: (i, 0)),
        ],
        out_shape=[
            jax.ShapeDtypeStruct((SEQ, D_PROJ), BF16),
            jax.ShapeDtypeStruct((SEQ, LANES), F32),
        ],
        scratch_shapes=[
            pltpu.VMEM((TM_IN, D_MODEL), BF16),
            pltpu.VMEM((D_XBC // TN_IN, HALO, TN_IN), F32),
        ],
        compiler_params=_params("arbitrary", "arbitrary"),
        name="in_proj",
    )(x, g, w, wdt, conv_w, conv_b)


def _gmlp_kernel(u_ref, v_ref, lng_ref, lnb_ref, ws_ref, bs_ref, ng_ref, ya_ref, wsm_scr, acc_scr):
    @pl.when(pl.program_id(0) == 0)
    def _():
        row = lax.broadcasted_iota(jnp.int32, (CHUNK, CHUNK), 0)
        col = lax.broadcasted_iota(jnp.int32, (CHUNK, CHUNK), 1)
        for g in range(N_GROUPS_A):
            wsm_scr[g] = jnp.where(row >= col, ws_ref[g], 0.0).astype(BF16)

    gi = lax.broadcasted_iota(jnp.int32, (2 * D_HEAD_A, 2 * D_HEAD_A), 0) // D_HEAD_A
    gj = lax.broadcasted_iota(jnp.int32, (2 * D_HEAD_A, 2 * D_HEAD_A), 1) // D_HEAD_A
    mean_m = jnp.where(gi == gj, 1.0 / D_HEAD_A, 0.0).astype(BF16)

    def group_mean(x_hi, x_lo=None):
        m = jnp.dot(x_hi, mean_m, preferred_element_type=F32)
        if x_lo is not None:
            m = m + jnp.dot(x_lo, mean_m, preferred_element_type=F32)
        return m

    def chunk_body(c, carry):
        r0 = pl.multiple_of(c * CHUNK, CHUNK)
        rows = pl.ds(r0, CHUNK)
        pairs = [slice(gp * 2 * D_HEAD_A, (gp + 1) * 2 * D_HEAD_A) for gp in range(N_GROUPS_A // 2)]
        vbs = [v_ref[rows, cols2] for cols2 in pairs]
        vcs = [vb.astype(F32) - group_mean(vb) for vb in vbs]
        sqs = [vc * vc for vc in vcs]
        sq_his = [sq.astype(BF16) for sq in sqs]
        vars_ = [group_mean(hi, (sq - hi.astype(F32)).astype(BF16)) for sq, hi in zip(sqs, sq_his)]
        vns = [(vc * lax.rsqrt(var + EPS) * lng_ref[:, cols2] + lnb_ref[:, cols2]).astype(BF16)
               for vc, var, cols2 in zip(vcs, vars_, pairs)]
        ssq = jnp.zeros((CHUNK, D_HEAD_A), F32)
        for g in range(N_GROUPS_A):
            cols = slice(g * D_HEAD_A, (g + 1) * D_HEAD_A)
            vn = vns[g // 2][:, (g % 2) * D_HEAD_A:(g % 2 + 1) * D_HEAD_A]
            sg = jnp.dot(wsm_scr[g], vn, preferred_element_type=F32) + bs_ref[g]
            ya = u_ref[rows, cols].astype(F32) * sg
            ssq = ssq + ya * ya
            acc_scr[rows, cols] = ya
        inv = lax.rsqrt(jnp.sum(ssq, axis=-1, keepdims=True) * (1.0 / D_A) + EPS)
        ya_ref[rows, :] = (acc_scr[rows, :] * inv * ng_ref[...]).astype(BF16)
        return carry

    lax.fori_loop(0, TM_GMLP // CHUNK, chunk_body, 0)


def _gmlp(proj, ln_g, ln_b, w_s, bs_full, norm_g):
    return pl.pallas_call(
        _gmlp_kernel,
        grid=(SEQ // TM_GMLP,),
        in_specs=[
            pl.BlockSpec((TM_GMLP, D_A), lambda i: (i, 0)),
            pl.BlockSpec((TM_GMLP, D_A), lambda i: (i, 1)),
            pl.BlockSpec((1, D_A), lambda i: (0, 0)),
            pl.BlockSpec((1, D_A), lambda i: (0, 0)),
            pl.BlockSpec((N_GROUPS_A, CHUNK, CHUNK), lambda i: (0, 0, 0)),
            pl.BlockSpec((N_GROUPS_A, CHUNK, D_HEAD_A), lambda i: (0, 0, 0)),
            pl.BlockSpec((1, D_A), lambda i: (0, 0)),
        ],
        out_specs=pl.BlockSpec((TM_GMLP, D_A), lambda i: (i, 0)),
        out_shape=jax.ShapeDtypeStruct((SEQ, D_A), BF16),
        scratch_shapes=[
            pltpu.VMEM((N_GROUPS_A, CHUNK, CHUNK), BF16),
            pltpu.VMEM((TM_GMLP, D_A), F32),
        ],
        compiler_params=_params("arbitrary"),
        name="gmlp",
    )(proj, proj, ln_g, ln_b, w_s, bs_full, norm_g)


def _ssd_kernel(zs_ref, xs_ref, bc_ref, dt_ref, dtb_ref, alog_ref, dskip_ref, ng_ref, yb_ref,
                h_scr, cs2_scr, srct_scr, w2t_scr):
    @pl.when(pl.program_id(0) == 0)
    def _():
        h_scr[...] = jnp.zeros((D_STATE, D_SSM), F32)

    row = lax.broadcasted_iota(jnp.int32, (CHUNK, CHUNK), 0)
    lane = lax.broadcasted_iota(jnp.int32, (CHUNK, CHUNK), 1)
    tril = row >= lane
    tri_b = jnp.where(tril, 1.0, 0.0).astype(BF16)
    left = lane < SSM_HEAD_DIM
    mask_l = jnp.where(left, 1.0, 0.0).astype(BF16)
    mask_r = jnp.where(left, 0.0, 1.0).astype(BF16)
    a_neg = -jnp.exp(alog_ref[...])

    dtv = dt_ref[...] + dtb_ref[...]
    dt_all = jnp.maximum(dtv, 0.0) + jnp.log1p(jnp.exp(-jnp.abs(dtv)))
    for c in range(R_SSD // CHUNK):
        crows = slice(c * CHUNK, (c + 1) * CHUNK)
        dt = dt_all[crows, :]
        adt = dt * a_neg
        hi = adt.astype(BF16)
        r1 = adt - hi.astype(F32)
        mid = r1.astype(BF16)
        lo = (r1 - mid.astype(F32)).astype(BF16)
        cs = (jnp.dot(tri_b, hi, preferred_element_type=F32)
              + jnp.dot(tri_b, mid, preferred_element_type=F32)
              + jnp.dot(tri_b, lo, preferred_element_type=F32))
        cs2 = cs * LOG2E
        cs2_scr[crows, :] = cs2
        srct_scr[crows, :] = (cs2 - jnp.log2(dt)).T
        w2t_scr[crows, :] = (dt * jnp.exp2(cs2[CHUNK - 1:CHUNK, :] - cs2)).T

    def chunk_body(c, carry):
        r0 = pl.multiple_of(c * CHUNK, CHUNK)
        rows = pl.ds(r0, CHUNK)
        cs2 = cs2_scr[rows, :]
        src_t = srct_scr[rows, :]
        w2_t = w2t_scr[rows, :]

        c_fs, bt_fs, scores_all = [], [], []
        for g in range(N_SSM_GROUPS):
            b_g = bc_ref[rows, g * D_STATE:(g + 1) * D_STATE]
            c_g = bc_ref[rows, N_SSM_GROUPS * D_STATE + g * D_STATE:
                         N_SSM_GROUPS * D_STATE + (g + 1) * D_STATE]
            bt_f = b_g.astype(F32).T
            c_fs.append(c_g.astype(F32))
            bt_fs.append(bt_f)
            scores_all.append(jnp.dot(c_g, bt_f.astype(BF16), preferred_element_type=F32))

        for g in range(N_SSM_GROUPS):
            c_f, bt_f, scores = c_fs[g], bt_fs[g], scores_all[g]
            ys = []
            ssq = jnp.zeros((CHUNK, 1), F32)
            for jp in range(2):
                pcols = slice((2 * g + jp) * LANES, (2 * g + jp + 1) * LANES)
                xp = xs_ref[rows, pcols]
                hp = h_scr[:, pcols]
                hpb = hp.astype(BF16)
                lhs_y, lhs_s, decay = [], [], []
                for hh in range(2):
                    h = g * HEADS_PER_GROUP + 2 * jp + hh
                    colv = jnp.sum(jnp.where(lane == h, cs2, 0.0), axis=-1, keepdims=True)
                    lmat = jnp.exp2(jnp.where(tril, colv - src_t[h:h + 1, :], -jnp.inf))
                    ecol = jnp.exp2(colv)
                    lhs_y += [(scores * lmat).astype(BF16), (c_f * ecol).astype(BF16)]
                    lhs_s.append((bt_f * w2_t[h:h + 1, :]).astype(BF16))
                    decay.append(jnp.exp2(colv[CHUNK - 1:CHUNK, :]))
                x_l, x_r = xp * mask_l, xp * mask_r
                yd = jnp.dot(jnp.concatenate(lhs_y, axis=1),
                             jnp.concatenate([x_l, hpb * mask_l, x_r, hpb * mask_r], axis=0),
                             preferred_element_type=F32)
                states = jnp.dot(jnp.concatenate(lhs_s, axis=1),
                                 jnp.concatenate([x_l, x_r], axis=0),
                                 preferred_element_type=F32)
                h_scr[:, pcols] = jnp.where(left, decay[0], decay[1]) * hp + states

                y = yd + dskip_ref[:, pcols] * xp.astype(F32)
                y = y * zs_ref[rows, pcols].astype(F32)
                ssq = ssq + jnp.sum(y * y, axis=-1, keepdims=True)
                ys.append((pcols, y))
            inv = lax.rsqrt(ssq * (1.0 / (2 * LANES)) + EPS)
            for pcols, y in ys:
                yb_ref[rows, pcols] = (y * inv * ng_ref[:, pcols]).astype(BF16)
        return carry

    lax.fori_loop(0, R_SSD // CHUNK, chunk_body, 0)


def _ssd(proj, dt_raw, dt_bias, a_log, d_skip_e, norm_g):
    blk = lambda col: pl.BlockSpec((R_SSD, D_SSM), lambda i: (i, col))
    full = lambda shape: pl.BlockSpec(shape, lambda i: (0,) * len(shape))
    return pl.pallas_call(
        _ssd_kernel,
        grid=(SEQ // R_SSD,),
        in_specs=[
            blk(2), blk(3), blk(4),
            pl.BlockSpec((R_SSD, LANES), lambda i: (i, 0)),
            full((1, LANES)), full((1, LANES)), full((1, D_SSM)), full((1, D_SSM)),
        ],
        out_specs=pl.BlockSpec((R_SSD, D_SSM), lambda i: (i, 0)),
        out_shape=jax.ShapeDtypeStruct((SEQ, D_SSM), BF16),
        scratch_shapes=[pltpu.VMEM((D_STATE, D_SSM), F32)] + [pltpu.VMEM((R_SSD, LANES), F32)] * 3,
        compiler_params=_params("arbitrary"),
        name="ssd",
    )(proj, proj, proj, dt_raw, dt_bias, a_log, d_skip_e, norm_g)


def _resident_proj_res_norm_kernel(n_lhs, tm, *refs):
    lhs_refs = refs[:n_lhs]
    w_ref, res_ref, g_ref, h_ref, n_ref = refs[n_lhs:]
    for r in range(tm // SUB_M):
        rows = slice(r * SUB_M, (r + 1) * SUB_M)
        h = res_ref[rows, :]
        k0 = 0
        for lhs in lhs_refs:
            k1 = k0 + lhs.shape[1]
            h = h + jnp.dot(lhs[rows, :], w_ref[k0:k1, :], preferred_element_type=F32)
            k0 = k1
        h_ref[rows, :] = h
        n_ref[rows, :] = (h * _rms_scale(h) * g_ref[...]).astype(BF16)


def _resident_proj_res_norm(lhs_list, w, res, g, tm, name):
    n_out = w.shape[1]
    in_specs = [pl.BlockSpec((tm, l.shape[1]), lambda i: (i, 0)) for l in lhs_list]
    in_specs += [
        pl.BlockSpec(w.shape, lambda i: (0, 0), pipeline_mode=pl.Buffered(1)),
        pl.BlockSpec((tm, n_out), lambda i: (i, 0)),
        pl.BlockSpec((1, n_out), lambda i: (0, 0)),
    ]
    return pl.pallas_call(
        functools.partial(_resident_proj_res_norm_kernel, len(lhs_list), tm),
        grid=(SEQ // tm,),
        in_specs=in_specs,
        out_specs=[
            pl.BlockSpec((tm, n_out), lambda i: (i, 0)),
            pl.BlockSpec((tm, n_out), lambda i: (i, 0)),
        ],
        out_shape=[
            jax.ShapeDtypeStruct((SEQ, n_out), F32),
            jax.ShapeDtypeStruct((SEQ, n_out), BF16),
        ],
        compiler_params=_params("arbitrary"),
        name=name,
    )(*lhs_list, w, res, g)


def _up_kernel(f_ref, wg_ref, wu_ref, cwg_ref, cwu_ref, cbg_ref, cbu_ref, act_ref, halo_scr):
    i = pl.program_id(0)
    j = pl.program_id(1)

    @pl.when(i == 0)
    def _():
        halo_scr[j] = jnp.zeros((HALO, 2 * TN_UP), F32)

    def conv(f, hist, w_ref, cw_ref, cb_ref):
        y = jnp.dot(f, w_ref[...].astype(BF16), preferred_element_type=F32)
        full = jnp.concatenate([hist, y], axis=0)
        acc = cb_ref[...] + cw_ref[FFN_CONV - 1:FFN_CONV, :] * y
        for k in range(FFN_CONV - 1):
            shifted = pltpu.roll(full, FFN_CONV - 1 - k, axis=0)[HALO:, :]
            acc = acc + cw_ref[k:k + 1, :] * shifted
        return acc, y[y.shape[0] - HALO:, :]

    gcols, ucols = slice(0, TN_UP), slice(TN_UP, 2 * TN_UP)
    hist_g, hist_u = halo_scr[j, :, gcols], halo_scr[j, :, ucols]
    for r in range(TM_UP // SUB_UP):
        rows = slice(r * SUB_UP, (r + 1) * SUB_UP)
        f = f_ref[rows, :]
        gate, hist_g = conv(f, hist_g, wg_ref, cwg_ref, cbg_ref)
        up, hist_u = conv(f, hist_u, wu_ref, cwu_ref, cbu_ref)
        act_ref[rows, :] = (gate * _sigmoid(gate) * up).astype(BF16)
    halo_scr[j, :, gcols] = hist_g
    halo_scr[j, :, ucols] = hist_u


def _up_proj(f, w_up, conv_w, conv_b):
    n_t = D_FF // TN_UP
    return pl.pallas_call(
        _up_kernel,
        grid=(SEQ // TM_UP, n_t),
        in_specs=[
            pl.BlockSpec((TM_UP, D_MODEL), lambda i, j: (i, 0)),
            pl.BlockSpec((D_MODEL, TN_UP), lambda i, j: (0, j)),
            pl.BlockSpec((D_MODEL, TN_UP), lambda i, j: (0, j + n_t)),
            pl.BlockSpec((FFN_CONV, TN_UP), lambda i, j: (0, j)),
            pl.BlockSpec((FFN_CONV, TN_UP), lambda i, j: (0, j + n_t)),
            pl.BlockSpec((1, TN_UP), lambda i, j: (0, j)),
            pl.BlockSpec((1, TN_UP), lambda i, j: (0, j + n_t)),
        ],
        out_specs=pl.BlockSpec((TM_UP, TN_UP), lambda i, j: (i, j)),
        out_shape=jax.ShapeDtypeStruct((SEQ, D_FF), BF16),
        scratch_shapes=[pltpu.VMEM((n_t, HALO, 2 * TN_UP), F32)],
        compiler_params=_params("arbitrary", "arbitrary"),
        name="up_proj",
    )(f, w_up, w_up, conv_w, conv_w, conv_b, conv_b)


def _down_ple_kernel(act_ref, wd_ref, h1_ref, gp_ref, wpg_ref, p_ref, wple_ref, gf_ref, o_ref):
    emb = jnp.dot(p_ref[...].astype(BF16), wple_ref[...], preferred_element_type=F32)
    h2 = h1_ref[...] + jnp.dot(act_ref[...], wd_ref[...], preferred_element_type=F32)
    n3 = (h2 * _rms_scale(h2) * gp_ref[...]).astype(BF16)
    gate = _sigmoid(jnp.dot(n3, wpg_ref[...], preferred_element_type=F32))
    h3 = h2 + gate * emb
    o_ref[...] = h3 * _rms_scale(h3) * gf_ref[...]


def _down_ple(act, w_down, h1, g_ple, w_pg, p, w_ple, g_final):
    rows = lambda width: pl.BlockSpec((TM_DOWN, width), lambda i: (i, 0))
    whole = lambda a: pl.BlockSpec(a.shape, lambda i: (0, 0), pipeline_mode=pl.Buffered(1))
    vec = pl.BlockSpec((1, D_MODEL), lambda i: (0, 0))
    return pl.pallas_call(
        _down_ple_kernel,
        grid=(SEQ // TM_DOWN,),
        in_specs=[rows(D_FF), whole(w_down), rows(D_MODEL), vec, whole(w_pg), rows(D_PLE),
                  whole(w_ple), vec],
        out_specs=rows(D_MODEL),
        out_shape=jax.ShapeDtypeStruct((SEQ, D_MODEL), F32),
        compiler_params=_params("arbitrary"),
        name="down_ple",
    )(act, w_down, h1, g_ple, w_pg, p, w_ple, g_final)


def _layer(h, p_i, norm_mix_g, w_in, ln_a_g, ln_a_b, w_s, b_s, norm_a_g, conv_ssm_w, conv_ssm_b,
           dt_bias, a_log, d_skip, ssm_norm_g, w_out, norm_ffn_g, w_up, conv_ffn_w, conv_ffn_b,
           w_down, norm_ple_g, w_ple_gate, w_ple, norm_final_g):
    row = lambda v: v.reshape(1, -1)
    pad_lanes = lambda v: jnp.pad(v, (0, LANES - v.shape[0])).reshape(1, LANES)

    w_in_t = jnp.swapaxes(w_in, 0, 1)
    w_dt_t = jnp.pad(w_in_t[D_PROJ:], ((0, LANES - N_SSM_HEADS), (0, 0))).astype(BF16)
    proj, dt_raw = _in_proj(h, row(norm_mix_g), w_in_t, w_dt_t, conv_ssm_w, row(conv_ssm_b))

    bs_full = jnp.broadcast_to(b_s[:, :, None], (N_GROUPS_A, CHUNK, D_HEAD_A))
    y_a = _gmlp(proj, row(ln_a_g), row(ln_a_b), w_s, bs_full, row(norm_a_g))

    y_b = _ssd(proj, dt_raw, pad_lanes(dt_bias), pad_lanes(a_log),
               row(jnp.repeat(d_skip, SSM_HEAD_DIM)), row(ssm_norm_g))

    h1, f = _resident_proj_res_norm([y_a, y_b], w_out.astype(BF16), h, row(norm_ffn_g),
                                    TM_OUT, "out_proj")
    act = _up_proj(f, w_up, conv_ffn_w, row(conv_ffn_b))
    return _down_ple(act, w_down.astype(BF16), h1, row(norm_ple_g), w_ple_gate.astype(BF16), p_i,
                     w_ple.astype(BF16), row(norm_final_g))


def kernel(x, p, norm_mix_g, w_in, ln_a_g, ln_a_b, w_s, b_s, norm_a_g, conv_ssm_w, conv_ssm_b,
           dt_bias, a_log, d_skip, ssm_norm_g, w_out, norm_ffn_g, w_up, conv_ffn_w, conv_ffn_b,
           w_down, norm_ple_g, w_ple_gate, w_ple, norm_final_g):
    assert x.shape == (1, SEQ, D_MODEL) and w_in.shape[0] == 1
    out = _layer(
        x[0], p[0, 0], norm_mix_g[0], w_in[0], ln_a_g[0], ln_a_b[0], w_s[0], b_s[0], norm_a_g[0],
        conv_ssm_w[0], conv_ssm_b[0], dt_bias[0], a_log[0], d_skip[0], ssm_norm_g[0], w_out[0],
        norm_ffn_g[0], w_up[0], conv_ffn_w[0], conv_ffn_b[0], w_down[0], norm_ple_g[0],
        w_ple_gate[0], w_ple[0], norm_final_g)
    return out[None]
```

```python
import functools
import math

import jax
import jax.numpy as jnp
from jax import lax
from jax.experimental import pallas as pl
from jax.experimental.pallas import tpu as pltpu

F32 = jnp.float32
BF16 = jnp.bfloat16

D_MODEL = 2048
SEQ = 8192
D_A = 2048
CHUNK = 128
N_GROUPS_A = 16
D_HEAD_A = 128
D_SSM = 2048
SSM_HEAD_DIM = 64
N_SSM_HEADS = 32
N_SSM_GROUPS = 8
HEADS_PER_GROUP = 4
D_STATE = 128
SSM_CONV = 4
D_XBC = 4096
D_PROJ = 10240
D_FF = 5632
FFN_CONV = 3
D_PLE = 256
EPS = 1e-6
LOG2E = 1.0 / math.log(2.0)

LANES = 128
SUBLANES = 8
HALO = SUBLANES
VMEM_LIMIT_BYTES = 56 * 1024 * 1024

TM_IN = 1024
SUB_IN = 512
TN_IN = 512
N_PAIR_IN = D_XBC // TN_IN
assert N_PAIR_IN == (2 * D_A) // TN_IN and D_SSM % (2 * TN_IN) == 0
SUB_M = 256
TM_GMLP = 512
R_SSD = 512
TM_OUT = 512
TM_UP = 1024
SUB_UP = 1024
TN_UP = 512
TM_DOWN = 256


def _params(*sem):
    return pltpu.CompilerParams(dimension_semantics=sem, vmem_limit_bytes=VMEM_LIMIT_BYTES)


def _sigmoid(x):
    return 1.0 / (1.0 + jnp.exp(-x))


def _rms_scale(x):
    return lax.rsqrt(jnp.mean(x * x, axis=-1, keepdims=True) + EPS)


def _dot_nt(a, bt):
    return lax.dot_general(a, bt, (((1,), (1,)), ((), ())), preferred_element_type=F32)


def _in_proj_kernel(x_ref, g_ref, wa_ref, wb_ref, wdt_ref, cw_ref, cb_ref,
                    xbc_ref, uv_ref, z_ref, dt_ref, a_scr, halo_scr):
    i = pl.program_id(0)
    j = pl.program_id(1)
    t = jnp.minimum(j, N_PAIR_IN - 1)

    @pl.when(j == 0)
    def _():
        x = x_ref[...]
        a = (x * _rms_scale(x) * g_ref[...]).astype(BF16)
        a_scr[...] = a
        dt_ref[...] = _dot_nt(a, wdt_ref[...])

    @pl.when((i == 0) & (j < N_PAIR_IN))
    def _():
        halo_scr[t] = jnp.zeros((HALO, TN_IN), F32)

    @pl.when(j < N_PAIR_IN)
    def _():
        wa = wa_ref[...].astype(BF16)
        wb = wb_ref[...].astype(BF16)
        subs = [slice(r * SUB_IN, (r + 1) * SUB_IN) for r in range(TM_IN // SUB_IN)]
        ys = [_dot_nt(a_scr[rows, :], wa) for rows in subs]
        y_uvs = [_dot_nt(a_scr[rows, :], wb) for rows in subs]
        hist = halo_scr[t]
        for rows, y in zip(subs, ys):
            full = jnp.concatenate([hist, y], axis=0)
            hist = y[SUB_IN - HALO:, :]
            acc = cb_ref[...] + cw_ref[SSM_CONV - 1:SSM_CONV, :] * y
            for k in range(SSM_CONV - 1):
                shifted = pltpu.roll(full, SSM_CONV - 1 - k, axis=0)[HALO:, :]
                acc = acc + cw_ref[k:k + 1, :] * shifted
            xbc_ref[rows, :] = (acc * _sigmoid(acc)).astype(BF16)
        halo_scr[t] = hist
        for rows, y_uv in zip(subs, y_uvs):
            uv_ref[rows, :] = jax.nn.gelu(y_uv).astype(BF16)

    @pl.when(j >= N_PAIR_IN)
    def _():
        subs = [slice(r * SUB_IN, (r + 1) * SUB_IN) for r in range(TM_IN // SUB_IN)]
        for half, w_ref in enumerate((wa_ref, wb_ref)):
            w = w_ref[...].astype(BF16)
            for rows in subs:
                y = _dot_nt(a_scr[rows, :], w)
                z_ref[rows, half * TN_IN:(half + 1) * TN_IN] = (y * _sigmoid(y)).astype(BF16)


def _in_proj(x, g, w_t, wdt, conv_w, conv_b):
    z0 = (2 * D_A) // TN_IN
    xbc0 = (2 * D_A + D_SSM) // TN_IN
    pair = lambda j: jnp.minimum(j, N_PAIR_IN - 1)
    zstep = lambda j: jnp.maximum(j - N_PAIR_IN, 0)
    wa_idx = lambda i, j: (jnp.where(j < N_PAIR_IN, xbc0 + j, z0 + 2 * zstep(j)), 0)
    wb_idx = lambda i, j: (jnp.where(j < N_PAIR_IN, j, z0 + 2 * zstep(j) + 1), 0)
    return pl.pallas_call(
        _in_proj_kernel,
        grid=(SEQ // TM_IN, N_PAIR_IN + D_SSM // (2 * TN_IN)),
        in_specs=[
            pl.BlockSpec((TM_IN, D_MODEL), lambda i, j: (i, 0)),
            pl.BlockSpec((1, D_MODEL), lambda i, j: (0, 0)),
            pl.BlockSpec((TN_IN, D_MODEL), wa_idx),
            pl.BlockSpec((TN_IN, D_MODEL), wb_idx),
            pl.BlockSpec((LANES, D_MODEL), lambda i, j: (0, 0)),
            pl.BlockSpec((SSM_CONV, TN_IN), lambda i, j: (0, pair(j))),
            pl.BlockSpec((1, TN_IN), lambda i, j: (0, pair(j))),
        ],
        out_specs=[
            pl.BlockSpec((TM_IN, TN_IN), lambda i, j: (i, pair(j))),
            pl.BlockSpec((TM_IN, TN_IN), lambda i, j: (i, pair(j))),
            pl.BlockSpec((TM_IN, 2 * TN_IN), lambda i, j: (i, zstep(j))),
            pl.BlockSpec((TM_IN, LANES), lambda i, j: (i, 0)),
        ],
        out_shape=[
            jax.ShapeDtypeStruct((SEQ, D_XBC), BF16),
            jax.ShapeDtypeStruct((SEQ, 2 * D_A), BF16),
            jax.ShapeDtypeStruct((SEQ, D_SSM), BF16),
            jax.ShapeDtypeStruct((SEQ, LANES), F32),
        ],
        scratch_shapes=[
            pltpu.VMEM((TM_IN, D_MODEL), BF16),
            pltpu.VMEM((N_PAIR_IN, HALO, TN_IN), F32),
        ],
        compiler_params=_params("arbitrary", "arbitrary"),
        name="in_proj",
    )(x, g, w_t, w_t, wdt, conv_w, conv_b)


def _gmlp_kernel(u_ref, v_ref, lng_ref, lnb_ref, ws_ref, bs_ref, ng_ref, ya_ref, wsm_scr, acc_scr):
    @pl.when(pl.program_id(0) == 0)
    def _():
        row = lax.broadcasted_iota(jnp.int32, (CHUNK, CHUNK), 0)
        col = lax.broadcasted_iota(jnp.int32, (CHUNK, CHUNK), 1)
        for g in range(N_GROUPS_A):
            wsm_scr[g] = jnp.where(row >= col, ws_ref[g], 0.0).astype(BF16)

    gi = lax.broadcasted_iota(jnp.int32, (2 * D_HEAD_A, 2 * D_HEAD_A), 0) // D_HEAD_A
    gj = lax.broadcasted_iota(jnp.int32, (2 * D_HEAD_A, 2 * D_HEAD_A), 1) // D_HEAD_A
    mean_m = jnp.where(gi == gj, 1.0 / D_HEAD_A, 0.0).astype(BF16)

    def group_mean(x_hi, x_lo=None):
        m = jnp.dot(x_hi, mean_m, preferred_element_type=F32)
        if x_lo is not None:
            m = m + jnp.dot(x_lo, mean_m, preferred_element_type=F32)
        return m

    def chunk_body(c, carry):
        r0 = pl.multiple_of(c * CHUNK, CHUNK)
        rows = pl.ds(r0, CHUNK)
        pairs = [slice(gp * 2 * D_HEAD_A, (gp + 1) * 2 * D_HEAD_A) for gp in range(N_GROUPS_A // 2)]
        vbs = [v_ref[rows, cols2] for cols2 in pairs]
        vcs = [vb.astype(F32) - group_mean(vb) for vb in vbs]
        sqs = [vc * vc for vc in vcs]
        sq_his = [sq.astype(BF16) for sq in sqs]
        vars_ = [group_mean(hi, (sq - hi.astype(F32)).astype(BF16)) for sq, hi in zip(sqs, sq_his)]
        vns = [(vc * lax.rsqrt(var + EPS) * lng_ref[:, cols2] + lnb_ref[:, cols2]).astype(BF16)
               for vc, var, cols2 in zip(vcs, vars_, pairs)]
        ssq = jnp.zeros((CHUNK, D_HEAD_A), F32)
        for g in range(N_GROUPS_A):
            cols = slice(g * D_HEAD_A, (g + 1) * D_HEAD_A)
            vn = vns[g // 2][:, (g % 2) * D_HEAD_A:(g % 2 + 1) * D_HEAD_A]
            sg = jnp.dot(wsm_scr[g], vn, preferred_element_type=F32) + bs_ref[g]
            ya = u_ref[rows, cols].astype(F32) * sg
            ssq = ssq + ya * ya
            acc_scr[rows, cols] = ya
        inv = lax.rsqrt(jnp.sum(ssq, axis=-1, keepdims=True) * (1.0 / D_A) + EPS)
        ya_ref[rows, :] = (acc_scr[rows, :] * inv * ng_ref[...]).astype(BF16)
        return carry

    lax.fori_loop(0, TM_GMLP // CHUNK, chunk_body, 0)


def _gmlp(proj, ln_g, ln_b, w_s, bs_full, norm_g):
    return pl.pallas_call(
        _gmlp_kernel,
        grid=(SEQ // TM_GMLP,),
        in_specs=[
            pl.BlockSpec((TM_GMLP, D_A), lambda i: (i, 0)),
            pl.BlockSpec((TM_GMLP, D_A), lambda i: (i, 1)),
            pl.BlockSpec((1, D_A), lambda i: (0, 0)),
            pl.BlockSpec((1, D_A), lambda i: (0, 0)),
            pl.BlockSpec((N_GROUPS_A, CHUNK, CHUNK), lambda i: (0, 0, 0)),
            pl.BlockSpec((N_GROUPS_A, CHUNK, D_HEAD_A), lambda i: (0, 0, 0)),
            pl.BlockSpec((1, D_A), lambda i: (0, 0)),
        ],
        out_specs=pl.BlockSpec((TM_GMLP, D_A), lambda i: (i, 0)),
        out_shape=jax.ShapeDtypeStruct((SEQ, D_A), BF16),
        scratch_shapes=[
            pltpu.VMEM((N_GROUPS_A, CHUNK, CHUNK), BF16),
            pltpu.VMEM((TM_GMLP, D_A), F32),
        ],
        compiler_params=_params("arbitrary"),
        name="gmlp",
    )(proj, proj, ln_g, ln_b, w_s, bs_full, norm_g)


def _ssd_kernel(zs_ref, xs_ref, bc_ref, dt_ref, dtb_ref, alog_ref, dskip_ref, ng_ref, yb_ref,
                h_scr, cs2_scr, srct_scr, w2t_scr):
    @pl.when(pl.program_id(0) == 0)
    def _():
        h_scr[...] = jnp.zeros((D_STATE, D_SSM), F32)

    row = lax.broadcasted_iota(jnp.int32, (CHUNK, CHUNK), 0)
    lane = lax.broadcasted_iota(jnp.int32, (CHUNK, CHUNK), 1)
    tril = row >= lane
    tri_b = jnp.where(tril, 1.0, 0.0).astype(BF16)
    left = lane < SSM_HEAD_DIM
    mask_l = jnp.where(left, 1.0, 0.0).astype(BF16)
    mask_r = jnp.where(left, 0.0, 1.0).astype(BF16)
    a_neg = -jnp.exp(alog_ref[...])

    dtv = dt_ref[...] + dtb_ref[...]
    dt_all = jnp.maximum(dtv, 0.0) + jnp.log1p(jnp.exp(-jnp.abs(dtv)))
    for c in range(R_SSD // CHUNK):
        crows = slice(c * CHUNK, (c + 1) * CHUNK)
        dt = dt_all[crows, :]
        adt = dt * a_neg
        hi = adt.astype(BF16)
        r1 = adt - hi.astype(F32)
        mid = r1.astype(BF16)
        lo = (r1 - mid.astype(F32)).astype(BF16)
        cs = (jnp.dot(tri_b, hi, preferred_element_type=F32)
              + jnp.dot(tri_b, mid, preferred_element_type=F32)
              + jnp.dot(tri_b, lo, preferred_element_type=F32))
        cs2 = cs * LOG2E
        cs2_scr[crows, :] = cs2
        srct_scr[crows, :] = (cs2 - jnp.log2(dt)).T
        w2t_scr[crows, :] = (dt * jnp.exp2(cs2[CHUNK - 1:CHUNK, :] - cs2)).T

    def chunk_body(c, carry):
        r0 = pl.multiple_of(c * CHUNK, CHUNK)
        rows = pl.ds(r0, CHUNK)
        cs2 = cs2_scr[rows, :]
        src_t = srct_scr[rows, :]
        w2_t = w2t_scr[rows, :]

        c_fs, bt_fs, scores_all = [], [], []
        for g in range(N_SSM_GROUPS):
            b_g = bc_ref[rows, g * D_STATE:(g + 1) * D_STATE]
            c_g = bc_ref[rows, N_SSM_GROUPS * D_STATE + g * D_STATE:
                         N_SSM_GROUPS * D_STATE + (g + 1) * D_STATE]
            bt_f = b_g.astype(F32).T
            c_fs.append(c_g.astype(F32))
            bt_fs.append(bt_f)
            scores_all.append(jnp.dot(c_g, bt_f.astype(BF16), preferred_element_type=F32))

        for g in range(N_SSM_GROUPS):
            c_f, bt_f, scores = c_fs[g], bt_fs[g], scores_all[g]
            ys = []
            ssq = jnp.zeros((CHUNK, 1), F32)
            for jp in range(2):
                pcols = slice((2 * g + jp) * LANES, (2 * g + jp + 1) * LANES)
                xp = xs_ref[rows, pcols]
                hp = h_scr[:, pcols]
                hpb = hp.astype(BF16)
                lhs_y, lhs_s, decay = [], [], []
                for hh in range(2):
                    h = g * HEADS_PER_GROUP + 2 * jp + hh
                    colv = jnp.sum(jnp.where(lane == h, cs2, 0.0), axis=-1, keepdims=True)
                    lmat = jnp.exp2(jnp.where(tril, colv - src_t[h:h + 1, :], -jnp.inf))
                    ecol = jnp.exp2(colv)
                    lhs_y += [(scores * lmat).astype(BF16), (c_f * ecol).astype(BF16)]
                    lhs_s.append((bt_f * w2_t[h:h + 1, :]).astype(BF16))
                    decay.append(jnp.exp2(colv[CHUNK - 1:CHUNK, :]))
                x_l, x_r = xp * mask_l, xp * mask_r
                yd = jnp.dot(jnp.concatenate(lhs_y, axis=1),
                             jnp.concatenate([x_l, hpb * mask_l, x_r, hpb * mask_r], axis=0),
                             preferred_element_type=F32)
                states = jnp.dot(jnp.concatenate(lhs_s, axis=1),
                                 jnp.concatenate([x_l, x_r], axis=0),
                                 preferred_element_type=F32)
                h_scr[:, pcols] = jnp.where(left, decay[0], decay[1]) * hp + states

                y = yd + dskip_ref[:, pcols] * xp.astype(F32)
                y = y * zs_ref[rows, pcols].astype(F32)
                ssq = ssq + jnp.sum(y * y, axis=-1, keepdims=True)
                ys.append((pcols, y))
            inv = lax.rsqrt(ssq * (1.0 / (2 * LANES)) + EPS)
            for pcols, y in ys:
                yb_ref[rows, pcols] = (y * inv * ng_ref[:, pcols]).astype(BF16)
        return carry

    lax.fori_loop(0, R_SSD // CHUNK, chunk_body, 0)


def _ssd(zs, xbc, dt_raw, dt_bias, a_log, d_skip_e, norm_g):
    blk = lambda col: pl.BlockSpec((R_SSD, D_SSM), lambda i: (i, col))
    full = lambda shape: pl.BlockSpec(shape, lambda i: (0,) * len(shape))
    return pl.pallas_call(
        _ssd_kernel,
        grid=(SEQ // R_SSD,),
        in_specs=[
            blk(0), blk(0), blk(1),
            pl.BlockSpec((R_SSD, LANES), lambda i: (i, 0)),
            full((1, LANES)), full((1, LANES)), full((1, D_SSM)), full((1, D_SSM)),
        ],
        out_specs=pl.BlockSpec((R_SSD, D_SSM), lambda i: (i, 0)),
        out_shape=jax.ShapeDtypeStruct((SEQ, D_SSM), BF16),
        scratch_shapes=[pltpu.VMEM((D_STATE, D_SSM), F32)] + [pltpu.VMEM((R_SSD, LANES), F32)] * 3,
        compiler_params=_params("arbitrary"),
        name="ssd",
    )(zs, xbc, xbc, dt_raw, dt_bias, a_log, d_skip_e, norm_g)


def _resident_proj_res_norm_kernel(n_lhs, tm, *refs):
    lhs_refs = refs[:n_lhs]
    w_ref, res_ref, g_ref, h_ref, n_ref = refs[n_lhs:]
    for r in range(tm // SUB_M):
        rows = slice(r * SUB_M, (r + 1) * SUB_M)
        h = res_ref[rows, :]
        k0 = 0
        for lhs in lhs_refs:
            k1 = k0 + lhs.shape[1]
            h = h + jnp.dot(lhs[rows, :], w_ref[k0:k1, :], preferred_element_type=F32)
            k0 = k1
        h_ref[rows, :] = h
        n_ref[rows, :] = (h * _rms_scale(h) * g_ref[...]).astype(BF16)


def _resident_proj_res_norm(lhs_list, w, res, g, tm, name):
    n_out = w.shape[1]
    in_specs = [pl.BlockSpec((tm, l.shape[1]), lambda i: (i, 0)) for l in lhs_list]
    in_specs += [
        pl.BlockSpec(w.shape, lambda i: (0, 0), pipeline_mode=pl.Buffered(1)),
        pl.BlockSpec((tm, n_out), lambda i: (i, 0)),
        pl.BlockSpec((1, n_out), lambda i: (0, 0)),
    ]
    return pl.pallas_call(
        functools.partial(_resident_proj_res_norm_kernel, len(lhs_list), tm),
        grid=(SEQ // tm,),
        in_specs=in_specs,
        out_specs=[
            pl.BlockSpec((tm, n_out), lambda i: (i, 0)),
            pl.BlockSpec((tm, n_out), lambda i: (i, 0)),
        ],
        out_shape=[
            jax.ShapeDtypeStruct((SEQ, n_out), F32),
            jax.ShapeDtypeStruct((SEQ, n_out), BF16),
        ],
        compiler_params=_params("arbitrary"),
        name=name,
    )(*lhs_list, w, res, g)


def _up_kernel(f_ref, wg_ref, wu_ref, cwg_ref, cwu_ref, cbg_ref, cbu_ref, act_ref, halo_scr):
    i = pl.program_id(0)
    j = pl.program_id(1)

    @pl.when(i == 0)
    def _():
        halo_scr[j] = jnp.zeros((HALO, 2 * TN_UP), F32)

    def conv(f, hist, w_ref, cw_ref, cb_ref):
        y = jnp.dot(f, w_ref[...].astype(BF16), preferred_element_type=F32)
        full = jnp.concatenate([hist, y], axis=0)
        acc = cb_ref[...] + cw_ref[FFN_CONV - 1:FFN_CONV, :] * y
        for k in range(FFN_CONV - 1):
            shifted = pltpu.roll(full, FFN_CONV - 1 - k, axis=0)[HALO:, :]
            acc = acc + cw_ref[k:k + 1, :] * shifted
        return acc, y[y.shape[0] - HALO:, :]

    gcols, ucols = slice(0, TN_UP), slice(TN_UP, 2 * TN_UP)
    hist_g, hist_u = halo_scr[j, :, gcols], halo_scr[j, :, ucols]
    for r in range(TM_UP // SUB_UP):
        rows = slice(r * SUB_UP, (r + 1) * SUB_UP)
        f = f_ref[rows, :]
        gate, hist_g = conv(f, hist_g, wg_ref, cwg_ref, cbg_ref)
        up, hist_u = conv(f, hist_u, wu_ref, cwu_ref, cbu_ref)
        act_ref[rows, :] = (gate * _sigmoid(gate) * up).astype(BF16)
    halo_scr[j, :, gcols] = hist_g
    halo_scr[j, :, ucols] = hist_u


def _up_proj(f, w_up, conv_w, conv_b):
    n_t = D_FF // TN_UP
    return pl.pallas_call(
        _up_kernel,
        grid=(SEQ // TM_UP, n_t),
        in_specs=[
            pl.BlockSpec((TM_UP, D_MODEL), lambda i, j: (i, 0)),
            pl.BlockSpec((D_MODEL, TN_UP), lambda i, j: (0, j)),
            pl.BlockSpec((D_MODEL, TN_UP), lambda i, j: (0, j + n_t)),
            pl.BlockSpec((FFN_CONV, TN_UP), lambda i, j: (0, j)),
            pl.BlockSpec((FFN_CONV, TN_UP), lambda i, j: (0, j + n_t)),
            pl.BlockSpec((1, TN_UP), lambda i, j: (0, j)),
            pl.BlockSpec((1, TN_UP), lambda i, j: (0, j + n_t)),
        ],
        out_specs=pl.BlockSpec((TM_UP, TN_UP), lambda i, j: (i, j)),
        out_shape=jax.ShapeDtypeStruct((SEQ, D_FF), BF16),
        scratch_shapes=[pltpu.VMEM((n_t, HALO, 2 * TN_UP), F32)],
        compiler_params=_params("arbitrary", "arbitrary"),
        name="up_proj",
    )(f, w_up, w_up, conv_w, conv_w, conv_b, conv_b)


def _down_ple_kernel(act_ref, wd_ref, h1_ref, gp_ref, wpg_ref, p_ref, wple_ref, gf_ref, o_ref):
    emb = jnp.dot(p_ref[...].astype(BF16), wple_ref[...], preferred_element_type=F32)
    h2 = h1_ref[...] + jnp.dot(act_ref[...], wd_ref[...], preferred_element_type=F32)
    n3 = (h2 * _rms_scale(h2) * gp_ref[...]).astype(BF16)
    gate = _sigmoid(jnp.dot(n3, wpg_ref[...], preferred_element_type=F32))
    h3 = h2 + gate * emb
    o_ref[...] = h3 * _rms_scale(h3) * gf_ref[...]


def _down_ple(act, w_down, h1, g_ple, w_pg, p, w_ple, g_final):
    rows = lambda width: pl.BlockSpec((TM_DOWN, width), lambda i: (i, 0))
    whole = lambda a: pl.BlockSpec(a.shape, lambda i: (0, 0), pipeline_mode=pl.Buffered(1))
    vec = pl.BlockSpec((1, D_MODEL), lambda i: (0, 0))
    return pl.pallas_call(
        _down_ple_kernel,
        grid=(SEQ // TM_DOWN,),
        in_specs=[rows(D_FF), whole(w_down), rows(D_MODEL), vec, whole(w_pg), rows(D_PLE),
                  whole(w_ple), vec],
        out_specs=rows(D_MODEL),
        out_shape=jax.ShapeDtypeStruct((SEQ, D_MODEL), F32),
        compiler_params=_params("arbitrary"),
        name="down_ple",
    )(act, w_down, h1, g_ple, w_pg, p, w_ple, g_final)


def _layer(h, p_i, norm_mix_g, w_in, ln_a_g, ln_a_b, w_s, b_s, norm_a_g, conv_ssm_w, conv_ssm_b,
           dt_bias, a_log, d_skip, ssm_norm_g, w_out, norm_ffn_g, w_up, conv_ffn_w, conv_ffn_b,
           w_down, norm_ple_g, w_ple_gate, w_ple, norm_final_g):
    row = lambda v: v.reshape(1, -1)
    pad_lanes = lambda v: jnp.pad(v, (0, LANES - v.shape[0])).reshape(1, LANES)

    w_in_t = jnp.swapaxes(w_in, 0, 1)
    w_dt_t = jnp.pad(w_in_t[D_PROJ:], ((0, LANES - N_SSM_HEADS), (0, 0))).astype(BF16)
    xbc, uv, zs, dt_raw = _in_proj(h, row(norm_mix_g), w_in_t, w_dt_t, conv_ssm_w, row(conv_ssm_b))

    bs_full = jnp.broadcast_to(b_s[:, :, None], (N_GROUPS_A, CHUNK, D_HEAD_A))
    y_a = _gmlp(uv, row(ln_a_g), row(ln_a_b), w_s, bs_full, row(norm_a_g))

    y_b = _ssd(zs, xbc, dt_raw, pad_lanes(dt_bias), pad_lanes(a_log),
               row(jnp.repeat(d_skip, SSM_HEAD_DIM)), row(ssm_norm_g))

    h1, f = _resident_proj_res_norm([y_a, y_b], w_out.astype(BF16), h, row(norm_ffn_g),
                                    TM_OUT, "out_proj")
    act = _up_proj(f, w_up, conv_ffn_w, row(conv_ffn_b))
    return _down_ple(act, w_down.astype(BF16), h1, row(norm_ple_g), w_ple_gate.astype(BF16), p_i,
                     w_ple.astype(BF16), row(norm_final_g))


def kernel(x, p, norm_mix_g, w_in, ln_a_g, ln_a_b, w_s, b_s, norm_a_g, conv_ssm_w, conv_ssm_b,
           dt_bias, a_log, d_skip, ssm_norm_g, w_out, norm_ffn_g, w_up, conv_ffn_w, conv_ffn_b,
           w_down, norm_ple_g, w_ple_gate, w_ple, norm_final_g):
    assert x.shape == (1, SEQ, D_MODEL) and w_in.shape[0] == 1
    out = _layer(
        x[0], p[0, 0], norm_mix_g[0], w_in[0], ln_a_g[0], ln_a_b[0], w_s[0], b_s[0], norm_a_g[0],
        conv_ssm_w[0], conv_ssm_b[0], dt_bias[0], a_log[0], d_skip[0], ssm_norm_g[0], w_out[0],
        norm_ffn_g[0], w_up[0], conv_ffn_w[0], conv_ffn_b[0], w_down[0], norm_ple_g[0],
        w_ple_gate[0], w_ple[0], norm_final_g)
    return out[None]
```

```python
import functools
import math

import jax
import jax.numpy as jnp
from jax import lax
from jax.experimental import pallas as pl
from jax.experimental.pallas import tpu as pltpu

F32 = jnp.float32
BF16 = jnp.bfloat16

D_MODEL = 2048
SEQ = 8192
D_A = 2048
CHUNK = 128
N_GROUPS_A = 16
D_HEAD_A = 128
D_SSM = 2048
SSM_HEAD_DIM = 64
N_SSM_HEADS = 32
N_SSM_GROUPS = 8
HEADS_PER_GROUP = 4
D_STATE = 128
SSM_CONV = 4
D_XBC = 4096
D_PROJ = 10240
D_FF = 5632
FFN_CONV = 3
D_PLE = 256
EPS = 1e-6
LOG2E = 1.0 / math.log(2.0)

LANES = 128
SUBLANES = 8
HALO = SUBLANES
VMEM_LIMIT_BYTES = 56 * 1024 * 1024

TM_IN = 1024
TN_IN = 1024
N_UV_TILES = (2 * D_A) // TN_IN
N_Z_END = (2 * D_A + D_SSM) // TN_IN
SUB_M = 256
TM_GMLP = 1024
R_SSD = 1024
TM_OUT = 512
W_CAST_ROWS = 512
TM_UP = 1024
TN_UP = 512
TM_DOWN = 256


def _params(*sem):
    return pltpu.CompilerParams(dimension_semantics=sem, vmem_limit_bytes=VMEM_LIMIT_BYTES)


def _sigmoid(x):
    return 1.0 / (1.0 + jnp.exp(-x))


def _gelu_tanh(x):
    inner = x * (1.0 + 0.044715 * (x * x))
    return x / (1.0 + jnp.exp2(inner * (-2.0 * math.sqrt(2.0 / math.pi) * LOG2E)))


def _rms_scale(x):
    return lax.rsqrt(jnp.mean(x * x, axis=-1, keepdims=True) + EPS)


def _causal_conv_rows(y, hist, cw_ref, cb_ref, width):
    full = jnp.concatenate([hist, y], axis=0)
    acc = cb_ref[...] + cw_ref[width - 1:width, :] * y
    for k in range(width - 1):
        acc = acc + cw_ref[k:k + 1, :] * pltpu.roll(full, width - 1 - k, axis=0)[HALO:, :]
    return acc, y[y.shape[0] - HALO:, :]


def _dot_nt(a, bt):
    return lax.dot_general(a, bt, (((1,), (1,)), ((), ())), preferred_element_type=F32)


def _in_proj_kernel(x_ref, g_ref, wt_ref, wdt_ref, cw_ref, cb_ref, proj_ref, dt_ref,
                    a_scr, halo_scr):
    i = pl.program_id(0)
    j = pl.program_id(1)
    t = jnp.maximum(j - N_Z_END, 0)

    @pl.when(j == 0)
    def _():
        x = x_ref[...]
        a = (x * _rms_scale(x) * g_ref[...]).astype(BF16)
        a_scr[...] = a
        dt_ref[...] = _dot_nt(a, wdt_ref[...])

    @pl.when((i == 0) & (j >= N_Z_END))
    def _():
        halo_scr[t] = jnp.zeros((HALO, TN_IN), F32)

    @pl.when(j < N_UV_TILES)
    def _():
        y = _dot_nt(a_scr[...], wt_ref[...].astype(BF16))
        proj_ref[...] = _gelu_tanh(y).astype(BF16)

    @pl.when((j >= N_UV_TILES) & (j < N_Z_END))
    def _():
        y = _dot_nt(a_scr[...], wt_ref[...].astype(BF16))
        proj_ref[...] = (y * _sigmoid(y)).astype(BF16)

    @pl.when(j >= N_Z_END)
    def _():
        y = _dot_nt(a_scr[...], wt_ref[...].astype(BF16))
        acc, halo_scr[t] = _causal_conv_rows(y, halo_scr[t], cw_ref, cb_ref, SSM_CONV)
        proj_ref[...] = (acc * _sigmoid(acc)).astype(BF16)


def _in_proj(x, g, w, wdt, conv_w, conv_b):
    xbc_tile = lambda i, j: (0, jnp.maximum(j - N_Z_END, 0))
    return pl.pallas_call(
        _in_proj_kernel,
        grid=(SEQ // TM_IN, D_PROJ // TN_IN),
        in_specs=[
            pl.BlockSpec((TM_IN, D_MODEL), lambda i, j: (i, 0)),
            pl.BlockSpec((1, D_MODEL), lambda i, j: (0, 0)),
            pl.BlockSpec((TN_IN, D_MODEL), lambda i, j: (j, 0)),
            pl.BlockSpec((LANES, D_MODEL), lambda i, j: (0, 0)),
            pl.BlockSpec((SSM_CONV, TN_IN), xbc_tile),
            pl.BlockSpec((1, TN_IN), xbc_tile),
        ],
        out_specs=[
            pl.BlockSpec((TM_IN, TN_IN), lambda i, j: (i, j)),
            pl.BlockSpec((TM_IN, LANES), lambda i, j: (i, 0)),
        ],
        out_shape=[
            jax.ShapeDtypeStruct((SEQ, D_PROJ), BF16),
            jax.ShapeDtypeStruct((SEQ, LANES), F32),
        ],
        scratch_shapes=[
            pltpu.VMEM((TM_IN, D_MODEL), BF16),
            pltpu.VMEM((D_XBC // TN_IN, HALO, TN_IN), F32),
        ],
        compiler_params=_params("arbitrary", "arbitrary"),
        name="in_proj",
    )(x, g, w, wdt, conv_w, conv_b)


def _gmlp_kernel(u_ref, v_ref, lng_ref, lnb_ref, ws_ref, bs_ref, ng_ref, ya_ref, wsm_scr, acc_scr):
    @pl.when(pl.program_id(0) == 0)
    def _():
        row = lax.broadcasted_iota(jnp.int32, (CHUNK, CHUNK), 0)
        col = lax.broadcasted_iota(jnp.int32, (CHUNK, CHUNK), 1)
        for g in range(N_GROUPS_A):
            wsm_scr[g] = jnp.where(row >= col, ws_ref[g], 0.0).astype(BF16)

    gi = lax.broadcasted_iota(jnp.int32, (2 * D_HEAD_A, 2 * D_HEAD_A), 0) // D_HEAD_A
    gj = lax.broadcasted_iota(jnp.int32, (2 * D_HEAD_A, 2 * D_HEAD_A), 1) // D_HEAD_A
    mean_m = jnp.where(gi == gj, 1.0 / D_HEAD_A, 0.0).astype(BF16)

    def group_mean(x_hi, x_lo=None):
        m = jnp.dot(x_hi, mean_m, preferred_element_type=F32)
        if x_lo is not None:
            m = m + jnp.dot(x_lo, mean_m, preferred_element_type=F32)
        return m

    def chunk_body(c, carry):
        r0 = pl.multiple_of(c * CHUNK, CHUNK)
        rows = pl.ds(r0, CHUNK)
        pairs = [slice(gp * 2 * D_HEAD_A, (gp + 1) * 2 * D_HEAD_A) for gp in range(N_GROUPS_A // 2)]
        vbs = [v_ref[rows, cols2] for cols2 in pairs]
        vcs = [vb.astype(F32) - group_mean(vb) for vb in vbs]
        sqs = [vc * vc for vc in vcs]
        sq_his = [sq.astype(BF16) for sq in sqs]
        vars_ = [group_mean(hi, (sq - hi.astype(F32)).astype(BF16)) for sq, hi in zip(sqs, sq_his)]
        vns = [(vc * lax.rsqrt(var + EPS) * lng_ref[:, cols2] + lnb_ref[:, cols2]).astype(BF16)
               for vc, var, cols2 in zip(vcs, vars_, pairs)]
        ssq = jnp.zeros((CHUNK, D_HEAD_A), F32)
        for g in range(N_GROUPS_A):
            cols = slice(g * D_HEAD_A, (g + 1) * D_HEAD_A)
            vn = vns[g // 2][:, (g % 2) * D_HEAD_A:(g % 2 + 1) * D_HEAD_A]
            sg = jnp.dot(wsm_scr[g], vn, preferred_element_type=F32) + bs_ref[g]
            ya = u_ref[rows, cols].astype(F32) * sg
            ssq = ssq + ya * ya
            acc_scr[rows, cols] = ya
        inv = lax.rsqrt(jnp.sum(ssq, axis=-1, keepdims=True) * (1.0 / D_A) + EPS)
        ya_ref[rows, :] = (acc_scr[rows, :] * inv * ng_ref[...]).astype(BF16)
        return carry

    lax.fori_loop(0, TM_GMLP // CHUNK, chunk_body, 0)


def _gmlp(proj, ln_g, ln_b, w_s, bs_full, norm_g):
    return pl.pallas_call(
        _gmlp_kernel,
        grid=(SEQ // TM_GMLP,),
        in_specs=[
            pl.BlockSpec((TM_GMLP, D_A), lambda i: (i, 0)),
            pl.BlockSpec((TM_GMLP, D_A), lambda i: (i, 1)),
            pl.BlockSpec((1, D_A), lambda i: (0, 0)),
            pl.BlockSpec((1, D_A), lambda i: (0, 0)),
            pl.BlockSpec((N_GROUPS_A, CHUNK, CHUNK), lambda i: (0, 0, 0)),
            pl.BlockSpec((N_GROUPS_A, CHUNK, D_HEAD_A), lambda i: (0, 0, 0)),
            pl.BlockSpec((1, D_A), lambda i: (0, 0)),
        ],
        out_specs=pl.BlockSpec((TM_GMLP, D_A), lambda i: (i, 0)),
        out_shape=jax.ShapeDtypeStruct((SEQ, D_A), BF16),
        scratch_shapes=[
            pltpu.VMEM((N_GROUPS_A, CHUNK, CHUNK), BF16),
            pltpu.VMEM((TM_GMLP, D_A), F32),
        ],
        compiler_params=_params("arbitrary"),
        name="gmlp",
    )(proj, proj, ln_g, ln_b, w_s, bs_full, norm_g)


def _ssd_kernel(zs_ref, xs_ref, bc_ref, dt_ref, dtb_ref, alog_ref, dskip_ref, ng_ref, yb_ref,
                h_scr, cs2_scr, srct_scr, w2t_scr):
    @pl.when(pl.program_id(0) == 0)
    def _():
        h_scr[...] = jnp.zeros((D_STATE, D_SSM), F32)

    row = lax.broadcasted_iota(jnp.int32, (CHUNK, CHUNK), 0)
    lane = lax.broadcasted_iota(jnp.int32, (CHUNK, CHUNK), 1)
    tril = row >= lane
    tri_b = jnp.where(tril, 1.0, 0.0).astype(BF16)
    left = lane < SSM_HEAD_DIM
    mask_l = jnp.where(left, 1.0, 0.0).astype(BF16)
    mask_r = jnp.where(left, 0.0, 1.0).astype(BF16)
    a_neg = -jnp.exp(alog_ref[...])

    dtv = dt_ref[...] + dtb_ref[...]
    dt_all = jnp.maximum(dtv, 0.0) + jnp.log1p(jnp.exp(-jnp.abs(dtv)))
    for c in range(R_SSD // CHUNK):
        crows = slice(c * CHUNK, (c + 1) * CHUNK)
        dt = dt_all[crows, :]
        adt = dt * a_neg
        hi = adt.astype(BF16)
        r1 = adt - hi.astype(F32)
        mid = r1.astype(BF16)
        lo = (r1 - mid.astype(F32)).astype(BF16)
        cs = (jnp.dot(tri_b, hi, preferred_element_type=F32)
              + jnp.dot(tri_b, mid, preferred_element_type=F32)
              + jnp.dot(tri_b, lo, preferred_element_type=F32))
        cs2 = cs * LOG2E
        cs2_scr[crows, :] = cs2
        srct_scr[crows, :] = (cs2 - jnp.log2(dt)).T
        w2t_scr[crows, :] = (dt * jnp.exp2(cs2[CHUNK - 1:CHUNK, :] - cs2)).T

    def chunk_body(c, carry):
        r0 = pl.multiple_of(c * CHUNK, CHUNK)
        rows = pl.ds(r0, CHUNK)
        cs2 = cs2_scr[rows, :]
        src_t = srct_scr[rows, :]
        w2_t = w2t_scr[rows, :]

        c_fs, bt_fs, scores_all = [], [], []
        for g in range(N_SSM_GROUPS):
            b_g = bc_ref[rows, g * D_STATE:(g + 1) * D_STATE]
            c_g = bc_ref[rows, N_SSM_GROUPS * D_STATE + g * D_STATE:
                         N_SSM_GROUPS * D_STATE + (g + 1) * D_STATE]
            bt_f = b_g.astype(F32).T
            c_fs.append(c_g.astype(F32))
            bt_fs.append(bt_f)
            scores_all.append(jnp.dot(c_g, bt_f.astype(BF16), preferred_element_type=F32))

        for g in range(N_SSM_GROUPS):
            c_f, bt_f, scores = c_fs[g], bt_fs[g], scores_all[g]
            ys = []
            ssq = jnp.zeros((CHUNK, 1), F32)
            for jp in range(2):
                pcols = slice((2 * g + jp) * LANES, (2 * g + jp + 1) * LANES)
                xp = xs_ref[rows, pcols]
                hp = h_scr[:, pcols]
                hpb = hp.astype(BF16)
                lhs_y, lhs_s, decay = [], [], []
                for hh in range(2):
                    h = g * HEADS_PER_GROUP + 2 * jp + hh
                    colv = jnp.sum(jnp.where(lane == h, cs2, 0.0), axis=-1, keepdims=True)
                    lmat = jnp.exp2(jnp.where(tril, colv - src_t[h:h + 1, :], -jnp.inf))
                    ecol = jnp.exp2(colv)
                    lhs_y += [(scores * lmat).astype(BF16), (c_f * ecol).astype(BF16)]
                    lhs_s.append((bt_f * w2_t[h:h + 1, :]).astype(BF16))
                    decay.append(jnp.exp2(colv[CHUNK - 1:CHUNK, :]))
                x_l, x_r = xp * mask_l, xp * mask_r
                yd = jnp.dot(jnp.concatenate(lhs_y, axis=1),
                             jnp.concatenate([x_l, hpb * mask_l, x_r, hpb * mask_r], axis=0),
                             preferred_element_type=F32)
                states = jnp.dot(jnp.concatenate(lhs_s, axis=1),
                                 jnp.concatenate([x_l, x_r], axis=0),
                                 preferred_element_type=F32)
                h_scr[:, pcols] = jnp.where(left, decay[0], decay[1]) * hp + states

                y = yd + dskip_ref[:, pcols] * xp.astype(F32)
                y = y * zs_ref[rows, pcols].astype(F32)
                ssq = ssq + jnp.sum(y * y, axis=-1, keepdims=True)
                ys.append((pcols, y))
            inv = lax.rsqrt(ssq * (1.0 / (2 * LANES)) + EPS)
            for pcols, y in ys:
                yb_ref[rows, pcols] = (y * inv * ng_ref[:, pcols]).astype(BF16)
        return carry

    lax.fori_loop(0, R_SSD // CHUNK, chunk_body, 0)


def _ssd(proj, dt_raw, dt_bias, a_log, d_skip_e, norm_g):
    blk = lambda col: pl.BlockSpec((R_SSD, D_SSM), lambda i: (i, col))
    full = lambda shape: pl.BlockSpec(shape, lambda i: (0,) * len(shape))
    return pl.pallas_call(
        _ssd_kernel,
        grid=(SEQ // R_SSD,),
        in_specs=[
            blk(2), blk(3), blk(4),
            pl.BlockSpec((R_SSD, LANES), lambda i: (i, 0)),
            full((1, LANES)), full((1, LANES)), full((1, D_SSM)), full((1, D_SSM)),
        ],
        out_specs=pl.BlockSpec((R_SSD, D_SSM), lambda i: (i, 0)),
        out_shape=jax.ShapeDtypeStruct((SEQ, D_SSM), BF16),
        scratch_shapes=[pltpu.VMEM((D_STATE, D_SSM), F32)] + [pltpu.VMEM((R_SSD, LANES), F32)] * 3,
        compiler_params=_params("arbitrary"),
        name="ssd",
    )(proj, proj, proj, dt_raw, dt_bias, a_log, d_skip_e, norm_g)


def _resident_proj_res_norm_kernel(n_lhs, tm, n_cast, *refs):
    lhs_refs = refs[:n_lhs]
    wf_ref, res_ref, g_ref, h_ref, n_ref, w_scr = refs[n_lhs:]
    s = pl.program_id(0)

    @pl.when(s < n_cast)
    def _():
        chunk = wf_ref.shape[0]
        w_scr[pl.ds(pl.multiple_of(s * chunk, chunk), chunk), :] = wf_ref[...].astype(BF16)

    @pl.when(s >= n_cast)
    def _():
        for r in range(tm // SUB_M):
            rows = slice(r * SUB_M, (r + 1) * SUB_M)
            h = res_ref[rows, :]
            k0 = 0
            for lhs in lhs_refs:
                k1 = k0 + lhs.shape[1]
                h = h + jnp.dot(lhs[rows, :], w_scr[k0:k1, :], preferred_element_type=F32)
                k0 = k1
            h_ref[rows, :] = h
            n_ref[rows, :] = (h * _rms_scale(h) * g_ref[...]).astype(BF16)


def _resident_proj_res_norm(lhs_list, w, res, g, tm, name):
    k_total, n_out = w.shape
    n_cast = k_total // W_CAST_ROWS
    blk = lambda s: (jnp.maximum(s - n_cast, 0), 0)
    in_specs = [pl.BlockSpec((tm, l.shape[1]), blk) for l in lhs_list]
    in_specs += [
        pl.BlockSpec((W_CAST_ROWS, n_out), lambda s: (jnp.minimum(s, n_cast - 1), 0)),
        pl.BlockSpec((tm, n_out), blk),
        pl.BlockSpec((1, n_out), lambda s: (0, 0)),
    ]
    return pl.pallas_call(
        functools.partial(_resident_proj_res_norm_kernel, len(lhs_list), tm, n_cast),
        grid=(n_cast + SEQ // tm,),
        in_specs=in_specs,
        out_specs=[pl.BlockSpec((tm, n_out), blk), pl.BlockSpec((tm, n_out), blk)],
        out_shape=[
            jax.ShapeDtypeStruct((SEQ, n_out), F32),
            jax.ShapeDtypeStruct((SEQ, n_out), BF16),
        ],
        scratch_shapes=[pltpu.VMEM((k_total, n_out), BF16)],
        compiler_params=_params("arbitrary"),
        name=name,
    )(*lhs_list, w, res, g)


def _up_kernel(f_ref, wg_ref, wu_ref, cwg_ref, cwu_ref, cbg_ref, cbu_ref, act_ref, halo_scr):
    i = pl.program_id(0)
    j = pl.program_id(1)

    @pl.when(i == 0)
    def _():
        halo_scr[j] = jnp.zeros((HALO, 2 * TN_UP), F32)

    gcols, ucols = slice(0, TN_UP), slice(TN_UP, 2 * TN_UP)
    f = f_ref[...]
    y_g = jnp.dot(f, wg_ref[...].astype(BF16), preferred_element_type=F32)
    y_u = jnp.dot(f, wu_ref[...].astype(BF16), preferred_element_type=F32)
    gate, halo_scr[j, :, gcols] = _causal_conv_rows(y_g, halo_scr[j, :, gcols], cwg_ref, cbg_ref,
                                                    FFN_CONV)
    up, halo_scr[j, :, ucols] = _causal_conv_rows(y_u, halo_scr[j, :, ucols], cwu_ref, cbu_ref,
                                                  FFN_CONV)
    act_ref[...] = (gate * _sigmoid(gate) * up).astype(BF16)


def _up_proj(f, w_up, conv_w, conv_b):
    n_t = D_FF // TN_UP
    return pl.pallas_call(
        _up_kernel,
        grid=(SEQ // TM_UP, n_t),
        in_specs=[
            pl.BlockSpec((TM_UP, D_MODEL), lambda i, j: (i, 0)),
            pl.BlockSpec((D_MODEL, TN_UP), lambda i, j: (0, j)),
            pl.BlockSpec((D_MODEL, TN_UP), lambda i, j: (0, j + n_t)),
            pl.BlockSpec((FFN_CONV, TN_UP), lambda i, j: (0, j)),
            pl.BlockSpec((FFN_CONV, TN_UP), lambda i, j: (0, j + n_t)),
            pl.BlockSpec((1, TN_UP), lambda i, j: (0, j)),
            pl.BlockSpec((1, TN_UP), lambda i, j: (0, j + n_t)),
        ],
        out_specs=pl.BlockSpec((TM_UP, TN_UP), lambda i, j: (i, j)),
        out_shape=jax.ShapeDtypeStruct((SEQ, D_FF), BF16),
        scratch_shapes=[pltpu.VMEM((n_t, HALO, 2 * TN_UP), F32)],
        compiler_params=_params("arbitrary", "arbitrary"),
        name="up_proj",
    )(f, w_up, w_up, conv_w, conv_w, conv_b, conv_b)


def _down_ple_kernel(act_ref, wd_ref, h1_ref, gp_ref, wpg_ref, p_ref, wple_ref, gf_ref, o_ref):
    emb = jnp.dot(p_ref[...].astype(BF16), wple_ref[...], preferred_element_type=F32)
    h2 = h1_ref[...] + jnp.dot(act_ref[...], wd_ref[...], preferred_element_type=F32)
    n3 = (h2 * _rms_scale(h2) * gp_ref[...]).astype(BF16)
    gate = _sigmoid(jnp.dot(n3, wpg_ref[...], preferred_element_type=F32))
    h3 = h2 + gate * emb
    o_ref[...] = h3 * _rms_scale(h3) * gf_ref[...]


def _down_ple(act, w_down, h1, g_ple, w_pg, p, w_ple, g_final):
    rows = lambda width: pl.BlockSpec((TM_DOWN, width), lambda i: (i, 0))
    whole = lambda a: pl.BlockSpec(a.shape, lambda i: (0, 0), pipeline_mode=pl.Buffered(1))
    vec = pl.BlockSpec((1, D_MODEL), lambda i: (0, 0))
    return pl.pallas_call(
        _down_ple_kernel,
        grid=(SEQ // TM_DOWN,),
        in_specs=[rows(D_FF), whole(w_down), rows(D_MODEL), vec, whole(w_pg), rows(D_PLE),
                  whole(w_ple), vec],
        out_specs=rows(D_MODEL),
        out_shape=jax.ShapeDtypeStruct((SEQ, D_MODEL), F32),
        compiler_params=_params("arbitrary"),
        name="down_ple",
    )(act, w_down, h1, g_ple, w_pg, p, w_ple, g_final)


def _layer(h, p_i, norm_mix_g, w_in, ln_a_g, ln_a_b, w_s, b_s, norm_a_g, conv_ssm_w, conv_ssm_b,
           dt_bias, a_log, d_skip, ssm_norm_g, w_out, norm_ffn_g, w_up, conv_ffn_w, conv_ffn_b,
           w_down, norm_ple_g, w_ple_gate, w_ple, norm_final_g):
    row = lambda v: v.reshape(1, -1)
    pad_lanes = lambda v: jnp.pad(v, (0, LANES - v.shape[0])).reshape(1, LANES)

    w_in_t = jnp.swapaxes(w_in, 0, 1)
    w_dt_t = jnp.pad(w_in_t[D_PROJ:], ((0, LANES - N_SSM_HEADS), (0, 0))).astype(BF16)
    proj, dt_raw = _in_proj(h, row(norm_mix_g), w_in_t, w_dt_t, conv_ssm_w, row(conv_ssm_b))

    bs_full = jnp.broadcast_to(b_s[:, :, None], (N_GROUPS_A, CHUNK, D_HEAD_A))
    y_a = _gmlp(proj, row(ln_a_g), row(ln_a_b), w_s, bs_full, row(norm_a_g))

    y_b = _ssd(proj, dt_raw, pad_lanes(dt_bias), pad_lanes(a_log),
               row(jnp.repeat(d_skip, SSM_HEAD_DIM)), row(ssm_norm_g))

    h1, f = _resident_proj_res_norm([y_a, y_b], w_out, h, row(norm_ffn_g),
                                    TM_OUT, "out_proj")
    act = _up_proj(f, w_up, conv_ffn_w, row(conv_ffn_b))
    return _down_ple(act, w_down.astype(BF16), h1, row(norm_ple_g), w_ple_gate.astype(BF16), p_i,
                     w_ple.astype(BF16), row(norm_final_g))


def kernel(x, p, norm_mix_g, w_in, ln_a_g, ln_a_b, w_s, b_s, norm_a_g, conv_ssm_w, conv_ssm_b,
           dt_bias, a_log, d_skip, ssm_norm_g, w_out, norm_ffn_g, w_up, conv_ffn_w, conv_ffn_b,
           w_down, norm_ple_g, w_ple_gate, w_ple, norm_final_g):
    assert x.shape == (1, SEQ, D_MODEL) and w_in.shape[0] == 1
    out = _layer(
        x[0], p[0, 0], norm_mix_g[0], w_in[0], ln_a_g[0], ln_a_b[0], w_s[0], b_s[0], norm_a_g[0],
        conv_ssm_w[0], conv_ssm_b[0], dt_bias[0], a_log[0], d_skip[0], ssm_norm_g[0], w_out[0],
        norm_ffn_g[0], w_up[0], conv_ffn_w[0], conv_ffn_b[0], w_down[0], norm_ple_g[0],
        w_ple_gate[0], w_ple[0], norm_final_g)
    return out[None]
```

```python
import functools
import math

import jax
import jax.numpy as jnp
from jax import lax
from jax.experimental import pallas as pl
from jax.experimental.pallas import tpu as pltpu

F32 = jnp.float32
BF16 = jnp.bfloat16

D_MODEL = 2048
SEQ = 8192
D_A = 2048
CHUNK = 128
N_GROUPS_A = 16
D_HEAD_A = 128
D_SSM = 2048
SSM_HEAD_DIM = 64
N_SSM_HEADS = 32
N_SSM_GROUPS = 8
HEADS_PER_GROUP = 4
D_STATE = 128
SSM_CONV = 4
D_XBC = 4096
D_PROJ = 10240
D_FF = 5632
FFN_CONV = 3
D_PLE = 256
EPS = 1e-6
LOG2E = 1.0 / math.log(2.0)

LANES = 128
SUBLANES = 8
HALO = SUBLANES
VMEM_LIMIT_BYTES = 56 * 1024 * 1024
VMEM_LIMIT_DOWN_BYTES = 60 * 1024 * 1024

TM_IN = 1024
TN_IN = 1024
N_UV_TILES = (2 * D_A) // TN_IN
N_Z_END = (2 * D_A + D_SSM) // TN_IN
SUB_M = 256
TM_GMLP = 512
R_SSD = 1024
TM_OUT = 512
W_CAST_ROWS = 512
TM_UP = 1024
TN_UP = 512
TM_DOWN = 256


def _params(*sem):
    return pltpu.CompilerParams(dimension_semantics=sem, vmem_limit_bytes=VMEM_LIMIT_BYTES)


def _sigmoid(x):
    return 1.0 / (1.0 + jnp.exp(-x))


def _gelu_tanh(x):
    inner = x * (1.0 + 0.044715 * (x * x))
    return x / (1.0 + jnp.exp2(inner * (-2.0 * math.sqrt(2.0 / math.pi) * LOG2E)))


def _rms_scale(x):
    return lax.rsqrt(jnp.mean(x * x, axis=-1, keepdims=True) + EPS)


def _causal_conv_rows(y, hist, cw_ref, cb_ref, width):
    full = jnp.concatenate([hist, y], axis=0)
    acc = cb_ref[...] + cw_ref[width - 1:width, :] * y
    for k in range(width - 1):
        acc = acc + cw_ref[k:k + 1, :] * pltpu.roll(full, width - 1 - k, axis=0)[HALO:, :]
    return acc, y[y.shape[0] - HALO:, :]


def _dot_nt(a, bt):
    return lax.dot_general(a, bt, (((1,), (1,)), ((), ())), preferred_element_type=F32)


def _in_proj_kernel(x_ref, g_ref, wt_ref, wdt_ref, cw_ref, cb_ref, proj_ref, dt_ref,
                    a_scr, halo_scr):
    i = pl.program_id(0)
    j = pl.program_id(1)
    t = jnp.maximum(j - N_Z_END, 0)

    @pl.when(j == 0)
    def _():
        x = x_ref[...]
        a = (x * _rms_scale(x) * g_ref[...]).astype(BF16)
        a_scr[...] = a
        dt_ref[...] = _dot_nt(a, wdt_ref[...])

    @pl.when((i == 0) & (j >= N_Z_END))
    def _():
        halo_scr[t] = jnp.zeros((HALO, TN_IN), F32)

    @pl.when(j < N_UV_TILES)
    def _():
        y = _dot_nt(a_scr[...], wt_ref[...].astype(BF16))
        proj_ref[...] = _gelu_tanh(y).astype(BF16)

    @pl.when((j >= N_UV_TILES) & (j < N_Z_END))
    def _():
        y = _dot_nt(a_scr[...], wt_ref[...].astype(BF16))
        proj_ref[...] = (y * _sigmoid(y)).astype(BF16)

    @pl.when(j >= N_Z_END)
    def _():
        y = _dot_nt(a_scr[...], wt_ref[...].astype(BF16))
        acc, halo_scr[t] = _causal_conv_rows(y, halo_scr[t], cw_ref, cb_ref, SSM_CONV)
        proj_ref[...] = (acc * _sigmoid(acc)).astype(BF16)


def _in_proj(x, g, w, wdt, conv_w, conv_b):
    xbc_tile = lambda i, j: (0, jnp.maximum(j - N_Z_END, 0))
    return pl.pallas_call(
        _in_proj_kernel,
        grid=(SEQ // TM_IN, D_PROJ // TN_IN),
        in_specs=[
            pl.BlockSpec((TM_IN, D_MODEL), lambda i, j: (i, 0)),
            pl.BlockSpec((1, D_MODEL), lambda i, j: (0, 0)),
            pl.BlockSpec((TN_IN, D_MODEL), lambda i, j: (j, 0)),
            pl.BlockSpec((LANES, D_MODEL), lambda i, j: (0, 0)),
            pl.BlockSpec((SSM_CONV, TN_IN), xbc_tile),
            pl.BlockSpec((1, TN_IN), xbc_tile),
        ],
        out_specs=[
            pl.BlockSpec((TM_IN, TN_IN), lambda i, j: (i, j)),
            pl.BlockSpec((TM_IN, LANES), lambda i, j: (i, 0)),
        ],
        out_shape=[
            jax.ShapeDtypeStruct((SEQ, D_PROJ), BF16),
            jax.ShapeDtypeStruct((SEQ, LANES), F32),
        ],
        scratch_shapes=[
            pltpu.VMEM((TM_IN, D_MODEL), BF16),
            pltpu.VMEM((D_XBC // TN_IN, HALO, TN_IN), F32),
        ],
        compiler_params=_params("arbitrary", "arbitrary"),
        name="in_proj",
    )(x, g, w, wdt, conv_w, conv_b)


def _gmlp_kernel(u_ref, v_ref, lng_ref, lnb_ref, ws_ref, bs_ref, ng_ref, ya_ref, wsm_scr, acc_scr):
    @pl.when(pl.program_id(0) == 0)
    def _():
        row = lax.broadcasted_iota(jnp.int32, (CHUNK, CHUNK), 0)
        col = lax.broadcasted_iota(jnp.int32, (CHUNK, CHUNK), 1)
        for g in range(N_GROUPS_A):
            wsm_scr[g] = jnp.where(row >= col, ws_ref[g], 0.0).astype(BF16)

    gi = lax.broadcasted_iota(jnp.int32, (2 * D_HEAD_A, 2 * D_HEAD_A), 0) // D_HEAD_A
    gj = lax.broadcasted_iota(jnp.int32, (2 * D_HEAD_A, 2 * D_HEAD_A), 1) // D_HEAD_A
    mean_m = jnp.where(gi == gj, 1.0 / D_HEAD_A, 0.0).astype(BF16)

    def group_mean(x_hi, x_lo=None):
        m = jnp.dot(x_hi, mean_m, preferred_element_type=F32)
        if x_lo is not None:
            m = m + jnp.dot(x_lo, mean_m, preferred_element_type=F32)
        return m

    def chunk_body(c, carry):
        r0 = pl.multiple_of(c * CHUNK, CHUNK)
        rows = pl.ds(r0, CHUNK)
        pairs = [slice(gp * 2 * D_HEAD_A, (gp + 1) * 2 * D_HEAD_A) for gp in range(N_GROUPS_A // 2)]
        vbs = [v_ref[rows, cols2] for cols2 in pairs]
        vcs = [vb.astype(F32) - group_mean(vb) for vb in vbs]
        sqs = [vc * vc for vc in vcs]
        sq_his = [sq.astype(BF16) for sq in sqs]
        vars_ = [group_mean(hi, (sq - hi.astype(F32)).astype(BF16)) for sq, hi in zip(sqs, sq_his)]
        vns = [(vc * lax.rsqrt(var + EPS) * lng_ref[:, cols2] + lnb_ref[:, cols2]).astype(BF16)
               for vc, var, cols2 in zip(vcs, vars_, pairs)]
        ssq = jnp.zeros((CHUNK, D_HEAD_A), F32)
        for g in range(N_GROUPS_A):
            cols = slice(g * D_HEAD_A, (g + 1) * D_HEAD_A)
            vn = vns[g // 2][:, (g % 2) * D_HEAD_A:(g % 2 + 1) * D_HEAD_A]
            sg = jnp.dot(wsm_scr[g], vn, preferred_element_type=F32) + bs_ref[g]
            ya = u_ref[rows, cols].astype(F32) * sg
            ssq = ssq + ya * ya
            acc_scr[rows, cols] = ya
        inv = lax.rsqrt(jnp.sum(ssq, axis=-1, keepdims=True) * (1.0 / D_A) + EPS)
        ya_ref[rows, :] = (acc_scr[rows, :] * inv * ng_ref[...]).astype(BF16)
        return carry

    lax.fori_loop(0, TM_GMLP // CHUNK, chunk_body, 0)


def _gmlp(proj, ln_g, ln_b, w_s, bs_full, norm_g):
    return pl.pallas_call(
        _gmlp_kernel,
        grid=(SEQ // TM_GMLP,),
        in_specs=[
            pl.BlockSpec((TM_GMLP, D_A), lambda i: (i, 0)),
            pl.BlockSpec((TM_GMLP, D_A), lambda i: (i, 1)),
            pl.BlockSpec((1, D_A), lambda i: (0, 0)),
            pl.BlockSpec((1, D_A), lambda i: (0, 0)),
            pl.BlockSpec((N_GROUPS_A, CHUNK, CHUNK), lambda i: (0, 0, 0)),
            pl.BlockSpec((N_GROUPS_A, CHUNK, D_HEAD_A), lambda i: (0, 0, 0)),
            pl.BlockSpec((1, D_A), lambda i: (0, 0)),
        ],
        out_specs=pl.BlockSpec((TM_GMLP, D_A), lambda i: (i, 0)),
        out_shape=jax.ShapeDtypeStruct((SEQ, D_A), BF16),
        scratch_shapes=[
            pltpu.VMEM((N_GROUPS_A, CHUNK, CHUNK), BF16),
            pltpu.VMEM((TM_GMLP, D_A), F32),
        ],
        compiler_params=_params("arbitrary"),
        name="gmlp",
    )(proj, proj, ln_g, ln_b, w_s, bs_full, norm_g)


def _ssd_kernel(zs_ref, xs_ref, bc_ref, dt_ref, dtb_ref, alog_ref, dskip_ref, ng_ref, yb_ref,
                h_scr, cs2_scr, srct_scr, w2t_scr):
    @pl.when(pl.program_id(0) == 0)
    def _():
        h_scr[...] = jnp.zeros((D_STATE, D_SSM), F32)

    row = lax.broadcasted_iota(jnp.int32, (CHUNK, CHUNK), 0)
    lane = lax.broadcasted_iota(jnp.int32, (CHUNK, CHUNK), 1)
    tril = row >= lane
    tri_b = jnp.where(tril, 1.0, 0.0).astype(BF16)
    left = lane < SSM_HEAD_DIM
    mask_l = jnp.where(left, 1.0, 0.0).astype(BF16)
    mask_r = jnp.where(left, 0.0, 1.0).astype(BF16)
    a_neg = -jnp.exp(alog_ref[...])

    dtv = dt_ref[...] + dtb_ref[...]
    dt_all = jnp.maximum(dtv, 0.0) + jnp.log1p(jnp.exp(-jnp.abs(dtv)))
    for c in range(R_SSD // CHUNK):
        crows = slice(c * CHUNK, (c + 1) * CHUNK)
        dt = dt_all[crows, :]
        adt = dt * a_neg
        hi = adt.astype(BF16)
        r1 = adt - hi.astype(F32)
        mid = r1.astype(BF16)
        lo = (r1 - mid.astype(F32)).astype(BF16)
        cs = (jnp.dot(tri_b, hi, preferred_element_type=F32)
              + jnp.dot(tri_b, mid, preferred_element_type=F32)
              + jnp.dot(tri_b, lo, preferred_element_type=F32))
        cs2 = cs * LOG2E
        cs2_scr[crows, :] = cs2
        srct_scr[crows, :] = (cs2 - jnp.log2(dt)).T
        w2t_scr[crows, :] = (dt * jnp.exp2(cs2[CHUNK - 1:CHUNK, :] - cs2)).T

    def chunk_body(c, carry):
        r0 = pl.multiple_of(c * CHUNK, CHUNK)
        rows = pl.ds(r0, CHUNK)
        cs2 = cs2_scr[rows, :]
        src_t = srct_scr[rows, :]
        w2_t = w2t_scr[rows, :]

        c_fs, bt_fs, scores_all = [], [], []
        for g in range(N_SSM_GROUPS):
            b_g = bc_ref[rows, g * D_STATE:(g + 1) * D_STATE]
            c_g = bc_ref[rows, N_SSM_GROUPS * D_STATE + g * D_STATE:
                         N_SSM_GROUPS * D_STATE + (g + 1) * D_STATE]
            bt_f = b_g.astype(F32).T
            c_fs.append(c_g.astype(F32))
            bt_fs.append(bt_f)
            scores_all.append(jnp.dot(c_g, bt_f.astype(BF16), preferred_element_type=F32))

        for g in range(N_SSM_GROUPS):
            c_f, bt_f, scores = c_fs[g], bt_fs[g], scores_all[g]
            ys = []
            ssq = jnp.zeros((CHUNK, 1), F32)
            for jp in range(2):
                pcols = slice((2 * g + jp) * LANES, (2 * g + jp + 1) * LANES)
                xp = xs_ref[rows, pcols]
                hp = h_scr[:, pcols]
                hpb = hp.astype(BF16)
                lhs_y, lhs_s, decay = [], [], []
                for hh in range(2):
                    h = g * HEADS_PER_GROUP + 2 * jp + hh
                    colv = jnp.sum(jnp.where(lane == h, cs2, 0.0), axis=-1, keepdims=True)
                    lmat = jnp.exp2(jnp.where(tril, colv - src_t[h:h + 1, :], -jnp.inf))
                    ecol = jnp.exp2(colv)
                    lhs_y += [(scores * lmat).astype(BF16), (c_f * ecol).astype(BF16)]
                    lhs_s.append((bt_f * w2_t[h:h + 1, :]).astype(BF16))
                    decay.append(jnp.exp2(colv[CHUNK - 1:CHUNK, :]))
                x_l, x_r = xp * mask_l, xp * mask_r
                yd = jnp.dot(jnp.concatenate(lhs_y, axis=1),
                             jnp.concatenate([x_l, hpb * mask_l, x_r, hpb * mask_r], axis=0),
                             preferred_element_type=F32)
                states = jnp.dot(jnp.concatenate(lhs_s, axis=1),
                                 jnp.concatenate([x_l, x_r], axis=0),
                                 preferred_element_type=F32)
                h_scr[:, pcols] = jnp.where(left, decay[0], decay[1]) * hp + states

                y = yd + dskip_ref[:, pcols] * xp.astype(F32)
                y = y * zs_ref[rows, pcols].astype(F32)
                ssq = ssq + jnp.sum(y * y, axis=-1, keepdims=True)
                ys.append((pcols, y))
            inv = lax.rsqrt(ssq * (1.0 / (2 * LANES)) + EPS)
            for pcols, y in ys:
                yb_ref[rows, pcols] = (y * inv * ng_ref[:, pcols]).astype(BF16)
        return carry

    lax.fori_loop(0, R_SSD // CHUNK, chunk_body, 0)


def _ssd(proj, dt_raw, dt_bias, a_log, d_skip_e, norm_g):
    blk = lambda col: pl.BlockSpec((R_SSD, D_SSM), lambda i: (i, col))
    full = lambda shape: pl.BlockSpec(shape, lambda i: (0,) * len(shape))
    return pl.pallas_call(
        _ssd_kernel,
        grid=(SEQ // R_SSD,),
        in_specs=[
            blk(2), blk(3), blk(4),
            pl.BlockSpec((R_SSD, LANES), lambda i: (i, 0)),
            full((1, LANES)), full((1, LANES)), full((1, D_SSM)), full((1, D_SSM)),
        ],
        out_specs=pl.BlockSpec((R_SSD, D_SSM), lambda i: (i, 0)),
        out_shape=jax.ShapeDtypeStruct((SEQ, D_SSM), BF16),
        scratch_shapes=[pltpu.VMEM((D_STATE, D_SSM), F32)] + [pltpu.VMEM((R_SSD, LANES), F32)] * 3,
        compiler_params=_params("arbitrary"),
        name="ssd",
    )(proj, proj, proj, dt_raw, dt_bias, a_log, d_skip_e, norm_g)


def _resident_proj_res_norm_kernel(n_lhs, tm, n_cast, *refs):
    lhs_refs = refs[:n_lhs]
    wf_ref, res_ref, g_ref, h_ref, n_ref, w_scr = refs[n_lhs:]
    s = pl.program_id(0)

    @pl.when(s < n_cast)
    def _():
        chunk = wf_ref.shape[0]
        w_scr[pl.ds(pl.multiple_of(s * chunk, chunk), chunk), :] = wf_ref[...].astype(BF16)

    @pl.when(s >= n_cast)
    def _():
        for r in range(tm // SUB_M):
            rows = slice(r * SUB_M, (r + 1) * SUB_M)
            h = res_ref[rows, :]
            k0 = 0
            for lhs in lhs_refs:
                k1 = k0 + lhs.shape[1]
                h = h + jnp.dot(lhs[rows, :], w_scr[k0:k1, :], preferred_element_type=F32)
                k0 = k1
            h_ref[rows, :] = h
            n_ref[rows, :] = (h * _rms_scale(h) * g_ref[...]).astype(BF16)


def _resident_proj_res_norm(lhs_list, w, res, g, tm, name):
    k_total, n_out = w.shape
    n_cast = k_total // W_CAST_ROWS
    blk = lambda s: (jnp.maximum(s - n_cast, 0), 0)
    in_specs = [pl.BlockSpec((tm, l.shape[1]), blk) for l in lhs_list]
    in_specs += [
        pl.BlockSpec((W_CAST_ROWS, n_out), lambda s: (jnp.minimum(s, n_cast - 1), 0)),
        pl.BlockSpec((tm, n_out), blk),
        pl.BlockSpec((1, n_out), lambda s: (0, 0)),
    ]
    return pl.pallas_call(
        functools.partial(_resident_proj_res_norm_kernel, len(lhs_list), tm, n_cast),
        grid=(n_cast + SEQ // tm,),
        in_specs=in_specs,
        out_specs=[pl.BlockSpec((tm, n_out), blk), pl.BlockSpec((tm, n_out), blk)],
        out_shape=[
            jax.ShapeDtypeStruct((SEQ, n_out), F32),
            jax.ShapeDtypeStruct((SEQ, n_out), BF16),
        ],
        scratch_shapes=[pltpu.VMEM((k_total, n_out), BF16)],
        compiler_params=_params("arbitrary"),
        name=name,
    )(*lhs_list, w, res, g)


def _up_kernel(f_ref, wg_ref, wu_ref, cwg_ref, cwu_ref, cbg_ref, cbu_ref, act_ref, halo_scr):
    i = pl.program_id(0)
    j = pl.program_id(1)

    @pl.when(i == 0)
    def _():
        halo_scr[j] = jnp.zeros((HALO, 2 * TN_UP), F32)

    gcols, ucols = slice(0, TN_UP), slice(TN_UP, 2 * TN_UP)
    f = f_ref[...]
    y_g = jnp.dot(f, wg_ref[...].astype(BF16), preferred_element_type=F32)
    y_u = jnp.dot(f, wu_ref[...].astype(BF16), preferred_element_type=F32)
    gate, halo_scr[j, :, gcols] = _causal_conv_rows(y_g, halo_scr[j, :, gcols], cwg_ref, cbg_ref,
                                                    FFN_CONV)
    up, halo_scr[j, :, ucols] = _causal_conv_rows(y_u, halo_scr[j, :, ucols], cwu_ref, cbu_ref,
                                                  FFN_CONV)
    act_ref[...] = (gate * _sigmoid(gate) * up).astype(BF16)


def _up_proj(f, w_up, conv_w, conv_b):
    n_t = D_FF // TN_UP
    return pl.pallas_call(
        _up_kernel,
        grid=(SEQ // TM_UP, n_t),
        in_specs=[
            pl.BlockSpec((TM_UP, D_MODEL), lambda i, j: (i, 0)),
            pl.BlockSpec((D_MODEL, TN_UP), lambda i, j: (0, j)),
            pl.BlockSpec((D_MODEL, TN_UP), lambda i, j: (0, j + n_t)),
            pl.BlockSpec((FFN_CONV, TN_UP), lambda i, j: (0, j)),
            pl.BlockSpec((FFN_CONV, TN_UP), lambda i, j: (0, j + n_t)),
            pl.BlockSpec((1, TN_UP), lambda i, j: (0, j)),
            pl.BlockSpec((1, TN_UP), lambda i, j: (0, j + n_t)),
        ],
        out_specs=pl.BlockSpec((TM_UP, TN_UP), lambda i, j: (i, j)),
        out_shape=jax.ShapeDtypeStruct((SEQ, D_FF), BF16),
        scratch_shapes=[pltpu.VMEM((n_t, HALO, 2 * TN_UP), F32)],
        compiler_params=_params("arbitrary", "arbitrary"),
        name="up_proj",
    )(f, w_up, w_up, conv_w, conv_w, conv_b, conv_b)


def _down_ple_kernel(n_cast, act_ref, wdf_ref, h1_ref, gp_ref, wpg_ref, p_ref, wple_ref, gf_ref,
                     o_ref, wd_scr):
    s = pl.program_id(0)

    @pl.when(s < n_cast)
    def _():
        chunk = wdf_ref.shape[0]
        wd_scr[pl.ds(pl.multiple_of(s * chunk, chunk), chunk), :] = wdf_ref[...].astype(BF16)

    @pl.when(s >= n_cast)
    def _():
        emb = jnp.dot(p_ref[...].astype(BF16), wple_ref[...], preferred_element_type=F32)
        h2 = h1_ref[...] + jnp.dot(act_ref[...], wd_scr[...], preferred_element_type=F32)
        n3 = (h2 * _rms_scale(h2) * gp_ref[...]).astype(BF16)
        gate = _sigmoid(jnp.dot(n3, wpg_ref[...], preferred_element_type=F32))
        h3 = h2 + gate * emb
        o_ref[...] = h3 * _rms_scale(h3) * gf_ref[...]


def _down_ple(act, w_down, h1, g_ple, w_pg, p, w_ple, g_final):
    n_cast = D_FF // W_CAST_ROWS
    blk = lambda s: (jnp.maximum(s - n_cast, 0), 0)
    rows = lambda width: pl.BlockSpec((TM_DOWN, width), blk)
    whole = lambda a: pl.BlockSpec(a.shape, lambda s: (0, 0), pipeline_mode=pl.Buffered(1))
    vec = pl.BlockSpec((1, D_MODEL), lambda s: (0, 0))
    wd_chunks = pl.BlockSpec((W_CAST_ROWS, D_MODEL), lambda s: (jnp.minimum(s, n_cast - 1), 0))
    return pl.pallas_call(
        functools.partial(_down_ple_kernel, n_cast),
        grid=(n_cast + SEQ // TM_DOWN,),
        in_specs=[rows(D_FF), wd_chunks, rows(D_MODEL), vec, whole(w_pg), rows(D_PLE),
                  whole(w_ple), vec],
        out_specs=rows(D_MODEL),
        out_shape=jax.ShapeDtypeStruct((SEQ, D_MODEL), F32),
        scratch_shapes=[pltpu.VMEM((D_FF, D_MODEL), BF16)],
        compiler_params=pltpu.CompilerParams(dimension_semantics=("arbitrary",),
                                             vmem_limit_bytes=VMEM_LIMIT_DOWN_BYTES),
        name="down_ple",
    )(act, w_down, h1, g_ple, w_pg, p, w_ple, g_final)


def _layer(h, p_i, norm_mix_g, w_in, ln_a_g, ln_a_b, w_s, b_s, norm_a_g, conv_ssm_w, conv_ssm_b,
           dt_bias, a_log, d_skip, ssm_norm_g, w_out, norm_ffn_g, w_up, conv_ffn_w, conv_ffn_b,
           w_down, norm_ple_g, w_ple_gate, w_ple, norm_final_g):
    row = lambda v: v.reshape(1, -1)
    pad_lanes = lambda v: jnp.pad(v, (0, LANES - v.shape[0])).reshape(1, LANES)

    w_in_t = jnp.swapaxes(w_in, 0, 1)
    w_dt_t = jnp.pad(w_in_t[D_PROJ:], ((0, LANES - N_SSM_HEADS), (0, 0))).astype(BF16)
    proj, dt_raw = _in_proj(h, row(norm_mix_g), w_in_t, w_dt_t, conv_ssm_w, row(conv_ssm_b))

    bs_full = jnp.broadcast_to(b_s[:, :, None], (N_GROUPS_A, CHUNK, D_HEAD_A))
    y_a = _gmlp(proj, row(ln_a_g), row(ln_a_b), w_s, bs_full, row(norm_a_g))

    y_b = _ssd(proj, dt_raw, pad_lanes(dt_bias), pad_lanes(a_log),
               row(jnp.repeat(d_skip, SSM_HEAD_DIM)), row(ssm_norm_g))

    h1, f = _resident_proj_res_norm([y_a, y_b], w_out, h, row(norm_ffn_g),
                                    TM_OUT, "out_proj")
    act = _up_proj(f, w_up, conv_ffn_w, row(conv_ffn_b))
    return _down_ple(act, w_down, h1, row(norm_ple_g), w_ple_gate.astype(BF16), p_i,
                     w_ple.astype(BF16), row(norm_final_g))


def kernel(x, p, norm_mix_g, w_in, ln_a_g, ln_a_b, w_s, b_s, norm_a_g, conv_ssm_w, conv_ssm_b,
           dt_bias, a_log, d_skip, ssm_norm_g, w_out, norm_ffn_g, w_up, conv_ffn_w, conv_ffn_b,
           w_down, norm_ple_g, w_ple_gate, w_ple, norm_final_g):
    assert x.shape == (1, SEQ, D_MODEL) and w_in.shape[0] == 1
    out = _layer(
        x[0], p[0, 0], norm_mix_g[0], w_in[0], ln_a_g[0], ln_a_b[0], w_s[0], b_s[0], norm_a_g[0],
        conv_ssm_w[0], conv_ssm_b[0], dt_bias[0], a_log[0], d_skip[0], ssm_norm_g[0], w_out[0],
        norm_ffn_g[0], w_up[0], conv_ffn_w[0], conv_ffn_b[0], w_down[0], norm_ple_g[0],
        w_ple_gate[0], w_ple[0], norm_final_g)
    return out[None]
```

```python
import functools
import math

import jax
import jax.numpy as jnp
from jax import lax
from jax.experimental import pallas as pl
from jax.experimental.pallas import tpu as pltpu

F32 = jnp.float32
BF16 = jnp.bfloat16

D_MODEL = 2048
SEQ = 8192
D_A = 2048
CHUNK = 128
N_GROUPS_A = 16
D_HEAD_A = 128
D_SSM = 2048
SSM_HEAD_DIM = 64
N_SSM_HEADS = 32
N_SSM_GROUPS = 8
HEADS_PER_GROUP = 4
D_STATE = 128
SSM_CONV = 4
D_XBC = 4096
D_PROJ = 10240
D_FF = 5632
FFN_CONV = 3
D_PLE = 256
EPS = 1e-6
LOG2E = 1.0 / math.log(2.0)

LANES = 128
SUBLANES = 8
HALO = SUBLANES
VMEM_LIMIT_BYTES = 56 * 1024 * 1024
VMEM_LIMIT_DOWN_BYTES = 60 * 1024 * 1024

TM_IN = 1024
TN_IN = 1024
N_UV_TILES = (2 * D_A) // TN_IN
N_Z_END = (2 * D_A + D_SSM) // TN_IN
SUB_M = 256
TM_GMLP = 512
R_SSD = 1024
TM_OUT = 512
W_CAST_ROWS = 512
TM_UP = 1024
TN_UP = 512
TM_DOWN = 256


def _params(*sem):
    return pltpu.CompilerParams(dimension_semantics=sem, vmem_limit_bytes=VMEM_LIMIT_BYTES)


def _sigmoid(x):
    return 1.0 / (1.0 + jnp.exp(-x))


def _gelu_tanh(x):
    inner = x * (1.0 + 0.044715 * (x * x))
    return x / (1.0 + jnp.exp2(inner * (-2.0 * math.sqrt(2.0 / math.pi) * LOG2E)))


def _rms_scale(x):
    return lax.rsqrt(jnp.mean(x * x, axis=-1, keepdims=True) + EPS)


def _causal_conv_rows(y, hist, cw_ref, cb_ref, width):
    full = jnp.concatenate([hist, y], axis=0)
    acc = cb_ref[...] + cw_ref[width - 1:width, :] * y
    for k in range(width - 1):
        acc = acc + cw_ref[k:k + 1, :] * pltpu.roll(full, width - 1 - k, axis=0)[HALO:, :]
    return acc, y[y.shape[0] - HALO:, :]


def _dot_nt(a, bt):
    return lax.dot_general(a, bt, (((1,), (1,)), ((), ())), preferred_element_type=F32)


def _in_proj_kernel(x_ref, g_ref, wt_ref, wdt_ref, cw_ref, cb_ref, proj_ref, dt_ref,
                    a_scr, halo_scr):
    i = pl.program_id(0)
    j = pl.program_id(1)
    t = jnp.maximum(j - N_Z_END, 0)

    @pl.when(j == 0)
    def _():
        x = x_ref[...]
        a = (x * _rms_scale(x) * g_ref[...]).astype(BF16)
        a_scr[...] = a
        dt_ref[...] = _dot_nt(a, wdt_ref[...])

    @pl.when((i == 0) & (j >= N_Z_END))
    def _():
        halo_scr[t] = jnp.zeros((HALO, TN_IN), F32)

    @pl.when(j < N_UV_TILES)
    def _():
        y = _dot_nt(a_scr[...], wt_ref[...].astype(BF16))
        proj_ref[...] = _gelu_tanh(y).astype(BF16)

    @pl.when((j >= N_UV_TILES) & (j < N_Z_END))
    def _():
        y = _dot_nt(a_scr[...], wt_ref[...].astype(BF16))
        proj_ref[...] = (y * _sigmoid(y)).astype(BF16)

    @pl.when(j >= N_Z_END)
    def _():
        y = _dot_nt(a_scr[...], wt_ref[...].astype(BF16))
        acc, halo_scr[t] = _causal_conv_rows(y, halo_scr[t], cw_ref, cb_ref, SSM_CONV)
        proj_ref[...] = (acc * _sigmoid(acc)).astype(BF16)


def _in_proj(x, g, w, wdt, conv_w, conv_b):
    xbc_tile = lambda i, j: (0, jnp.maximum(j - N_Z_END, 0))
    return pl.pallas_call(
        _in_proj_kernel,
        grid=(SEQ // TM_IN, D_PROJ // TN_IN),
        in_specs=[
            pl.BlockSpec((TM_IN, D_MODEL), lambda i, j: (i, 0)),
            pl.BlockSpec((1, D_MODEL), lambda i, j: (0, 0)),
            pl.BlockSpec((TN_IN, D_MODEL), lambda i, j: (j, 0)),
            pl.BlockSpec((LANES, D_MODEL), lambda i, j: (0, 0)),
            pl.BlockSpec((SSM_CONV, TN_IN), xbc_tile),
            pl.BlockSpec((1, TN_IN), xbc_tile),
        ],
        out_specs=[
            pl.BlockSpec((TM_IN, TN_IN), lambda i, j: (i, j)),
            pl.BlockSpec((TM_IN, LANES), lambda i, j: (i, 0)),
        ],
        out_shape=[
            jax.ShapeDtypeStruct((SEQ, D_PROJ), BF16),
            jax.ShapeDtypeStruct((SEQ, LANES), F32),
        ],
        scratch_shapes=[
            pltpu.VMEM((TM_IN, D_MODEL), BF16),
            pltpu.VMEM((D_XBC // TN_IN, HALO, TN_IN), F32),
        ],
        compiler_params=_params("arbitrary", "arbitrary"),
        name="in_proj",
    )(x, g, w, wdt, conv_w, conv_b)


def _gmlp_kernel(u_ref, v_ref, lng_ref, lnb_ref, ws_ref, bs_ref, ng_ref, ya_ref, wsm_scr, acc_scr):
    @pl.when(pl.program_id(0) == 0)
    def _():
        row = lax.broadcasted_iota(jnp.int32, (CHUNK, CHUNK), 0)
        col = lax.broadcasted_iota(jnp.int32, (CHUNK, CHUNK), 1)
        for g in range(N_GROUPS_A):
            wsm_scr[g] = jnp.where(row >= col, ws_ref[g], 0.0).astype(BF16)

    gi = lax.broadcasted_iota(jnp.int32, (2 * D_HEAD_A, 2 * D_HEAD_A), 0) // D_HEAD_A
    gj = lax.broadcasted_iota(jnp.int32, (2 * D_HEAD_A, 2 * D_HEAD_A), 1) // D_HEAD_A
    mean_m = jnp.where(gi == gj, 1.0 / D_HEAD_A, 0.0).astype(BF16)

    def group_mean(x):
        return jnp.dot(x, mean_m, preferred_element_type=F32)

    def chunk_body(c, carry):
        r0 = pl.multiple_of(c * CHUNK, CHUNK)
        rows = pl.ds(r0, CHUNK)
        pairs = [slice(gp * 2 * D_HEAD_A, (gp + 1) * 2 * D_HEAD_A) for gp in range(N_GROUPS_A // 2)]
        vbs = [v_ref[rows, cols2] for cols2 in pairs]
        vcs = [vb.astype(F32) - group_mean(vb) for vb in vbs]
        vars_ = [group_mean((vc * vc).astype(BF16)) for vc in vcs]
        vns = [(vc * lax.rsqrt(var + EPS) * lng_ref[:, cols2] + lnb_ref[:, cols2]).astype(BF16)
               for vc, var, cols2 in zip(vcs, vars_, pairs)]
        ssq = jnp.zeros((CHUNK, D_HEAD_A), F32)
        for g in range(N_GROUPS_A):
            cols = slice(g * D_HEAD_A, (g + 1) * D_HEAD_A)
            vn = vns[g // 2][:, (g % 2) * D_HEAD_A:(g % 2 + 1) * D_HEAD_A]
            sg = jnp.dot(wsm_scr[g], vn, preferred_element_type=F32) + bs_ref[g]
            ya = u_ref[rows, cols].astype(F32) * sg
            ssq = ssq + ya * ya
            acc_scr[rows, cols] = ya
        inv = lax.rsqrt(jnp.sum(ssq, axis=-1, keepdims=True) * (1.0 / D_A) + EPS)
        ya_ref[rows, :] = (acc_scr[rows, :] * inv * ng_ref[...]).astype(BF16)
        return carry

    lax.fori_loop(0, TM_GMLP // CHUNK, chunk_body, 0)


def _gmlp(proj, ln_g, ln_b, w_s, bs_full, norm_g):
    return pl.pallas_call(
        _gmlp_kernel,
        grid=(SEQ // TM_GMLP,),
        in_specs=[
            pl.BlockSpec((TM_GMLP, D_A), lambda i: (i, 0)),
            pl.BlockSpec((TM_GMLP, D_A), lambda i: (i, 1)),
            pl.BlockSpec((1, D_A), lambda i: (0, 0)),
            pl.BlockSpec((1, D_A), lambda i: (0, 0)),
            pl.BlockSpec((N_GROUPS_A, CHUNK, CHUNK), lambda i: (0, 0, 0)),
            pl.BlockSpec((N_GROUPS_A, CHUNK, D_HEAD_A), lambda i: (0, 0, 0)),
            pl.BlockSpec((1, D_A), lambda i: (0, 0)),
        ],
        out_specs=pl.BlockSpec((TM_GMLP, D_A), lambda i: (i, 0)),
        out_shape=jax.ShapeDtypeStruct((SEQ, D_A), BF16),
        scratch_shapes=[
            pltpu.VMEM((N_GROUPS_A, CHUNK, CHUNK), BF16),
            pltpu.VMEM((TM_GMLP, D_A), F32),
        ],
        compiler_params=_params("arbitrary"),
        name="gmlp",
    )(proj, proj, ln_g, ln_b, w_s, bs_full, norm_g)


def _ssd_kernel(zs_ref, xs_ref, bc_ref, dt_ref, dtb_ref, alog_ref, dskip_ref, ng_ref, yb_ref,
                h_scr, cs2_scr, srct_scr, w2t_scr):
    @pl.when(pl.program_id(0) == 0)
    def _():
        h_scr[...] = jnp.zeros((D_STATE, D_SSM), F32)

    row = lax.broadcasted_iota(jnp.int32, (CHUNK, CHUNK), 0)
    lane = lax.broadcasted_iota(jnp.int32, (CHUNK, CHUNK), 1)
    tril = row >= lane
    tri_b = jnp.where(tril, 1.0, 0.0).astype(BF16)
    left = lane < SSM_HEAD_DIM
    mask_l = jnp.where(left, 1.0, 0.0).astype(BF16)
    mask_r = jnp.where(left, 0.0, 1.0).astype(BF16)
    a_neg = -jnp.exp(alog_ref[...])

    dtv = dt_ref[...] + dtb_ref[...]
    dt_all = jnp.maximum(dtv, 0.0) + jnp.log1p(jnp.exp(-jnp.abs(dtv)))
    for c in range(R_SSD // CHUNK):
        crows = slice(c * CHUNK, (c + 1) * CHUNK)
        dt = dt_all[crows, :]
        adt = dt * a_neg
        hi = adt.astype(BF16)
        r1 = adt - hi.astype(F32)
        mid = r1.astype(BF16)
        lo = (r1 - mid.astype(F32)).astype(BF16)
        cs = (jnp.dot(tri_b, hi, preferred_element_type=F32)
              + jnp.dot(tri_b, mid, preferred_element_type=F32)
              + jnp.dot(tri_b, lo, preferred_element_type=F32))
        cs2 = cs * LOG2E
        cs2_scr[crows, :] = cs2
        srct_scr[crows, :] = (cs2 - jnp.log2(dt)).T
        w2t_scr[crows, :] = (dt * jnp.exp2(cs2[CHUNK - 1:CHUNK, :] - cs2)).T

    def chunk_body(c, carry):
        r0 = pl.multiple_of(c * CHUNK, CHUNK)
        rows = pl.ds(r0, CHUNK)
        cs2 = cs2_scr[rows, :]
        src_t = srct_scr[rows, :]
        w2_t = w2t_scr[rows, :]

        c_fs, bt_fs, scores_all = [], [], []
        for g in range(N_SSM_GROUPS):
            b_g = bc_ref[rows, g * D_STATE:(g + 1) * D_STATE]
            c_g = bc_ref[rows, N_SSM_GROUPS * D_STATE + g * D_STATE:
                         N_SSM_GROUPS * D_STATE + (g + 1) * D_STATE]
            bt_f = b_g.astype(F32).T
            c_fs.append(c_g.astype(F32))
            bt_fs.append(bt_f)
            scores_all.append(jnp.dot(c_g, bt_f.astype(BF16), preferred_element_type=F32))

        for g in range(N_SSM_GROUPS):
            c_f, bt_f, scores = c_fs[g], bt_fs[g], scores_all[g]
            ys = []
            ssq = jnp.zeros((CHUNK, 1), F32)
            for jp in range(2):
                pcols = slice((2 * g + jp) * LANES, (2 * g + jp + 1) * LANES)
                xp = xs_ref[rows, pcols]
                hp = h_scr[:, pcols]
                hpb = hp.astype(BF16)
                lhs_y, lhs_s, decay = [], [], []
                for hh in range(2):
                    h = g * HEADS_PER_GROUP + 2 * jp + hh
                    colv = jnp.sum(jnp.where(lane == h, cs2, 0.0), axis=-1, keepdims=True)
                    lmat = jnp.exp2(jnp.where(tril, colv - src_t[h:h + 1, :], -jnp.inf))
                    ecol = jnp.exp2(colv)
                    lhs_y += [(scores * lmat).astype(BF16), (c_f * ecol).astype(BF16)]
                    lhs_s.append((bt_f * w2_t[h:h + 1, :]).astype(BF16))
                    decay.append(jnp.exp2(colv[CHUNK - 1:CHUNK, :]))
                x_l, x_r = xp * mask_l, xp * mask_r
                yd = jnp.dot(jnp.concatenate(lhs_y, axis=1),
                             jnp.concatenate([x_l, hpb * mask_l, x_r, hpb * mask_r], axis=0),
                             preferred_element_type=F32)
                states = jnp.dot(jnp.concatenate(lhs_s, axis=1),
                                 jnp.concatenate([x_l, x_r], axis=0),
                                 preferred_element_type=F32)
                h_scr[:, pcols] = jnp.where(left, decay[0], decay[1]) * hp + states

                y = yd + dskip_ref[:, pcols] * xp.astype(F32)
                y = y * zs_ref[rows, pcols].astype(F32)
                ssq = ssq + jnp.sum(y * y, axis=-1, keepdims=True)
                ys.append((pcols, y))
            inv = lax.rsqrt(ssq * (1.0 / (2 * LANES)) + EPS)
            for pcols, y in ys:
                yb_ref[rows, pcols] = (y * inv * ng_ref[:, pcols]).astype(BF16)
        return carry

    lax.fori_loop(0, R_SSD // CHUNK, chunk_body, 0)


def _ssd(proj, dt_raw, dt_bias, a_log, d_skip_e, norm_g):
    blk = lambda col: pl.BlockSpec((R_SSD, D_SSM), lambda i: (i, col))
    full = lambda shape: pl.BlockSpec(shape, lambda i: (0,) * len(shape))
    return pl.pallas_call(
        _ssd_kernel,
        grid=(SEQ // R_SSD,),
        in_specs=[
            blk(2), blk(3), blk(4),
            pl.BlockSpec((R_SSD, LANES), lambda i: (i, 0)),
            full((1, LANES)), full((1, LANES)), full((1, D_SSM)), full((1, D_SSM)),
        ],
        out_specs=pl.BlockSpec((R_SSD, D_SSM), lambda i: (i, 0)),
        out_shape=jax.ShapeDtypeStruct((SEQ, D_SSM), BF16),
        scratch_shapes=[pltpu.VMEM((D_STATE, D_SSM), F32)] + [pltpu.VMEM((R_SSD, LANES), F32)] * 3,
        compiler_params=_params("arbitrary"),
        name="ssd",
    )(proj, proj, proj, dt_raw, dt_bias, a_log, d_skip_e, norm_g)


def _resident_proj_res_norm_kernel(n_lhs, tm, n_cast, *refs):
    lhs_refs = refs[:n_lhs]
    wf_ref, res_ref, g_ref, h_ref, n_ref, w_scr = refs[n_lhs:]
    s = pl.program_id(0)

    @pl.when(s < n_cast)
    def _():
        chunk = wf_ref.shape[0]
        w_scr[pl.ds(pl.multiple_of(s * chunk, chunk), chunk), :] = wf_ref[...].astype(BF16)

    @pl.when(s >= n_cast)
    def _():
        for r in range(tm // SUB_M):
            rows = slice(r * SUB_M, (r + 1) * SUB_M)
            h = res_ref[rows, :]
            k0 = 0
            for lhs in lhs_refs:
                k1 = k0 + lhs.shape[1]
                h = h + jnp.dot(lhs[rows, :], w_scr[k0:k1, :], preferred_element_type=F32)
                k0 = k1
            h_ref[rows, :] = h
            n_ref[rows, :] = (h * _rms_scale(h) * g_ref[...]).astype(BF16)


def _resident_proj_res_norm(lhs_list, w, res, g, tm, name):
    k_total, n_out = w.shape
    n_cast = k_total // W_CAST_ROWS
    blk = lambda s: (jnp.maximum(s - n_cast, 0), 0)
    in_specs = [pl.BlockSpec((tm, l.shape[1]), blk) for l in lhs_list]
    in_specs += [
        pl.BlockSpec((W_CAST_ROWS, n_out), lambda s: (jnp.minimum(s, n_cast - 1), 0)),
        pl.BlockSpec((tm, n_out), blk),
        pl.BlockSpec((1, n_out), lambda s: (0, 0)),
    ]
    return pl.pallas_call(
        functools.partial(_resident_proj_res_norm_kernel, len(lhs_list), tm, n_cast),
        grid=(n_cast + SEQ // tm,),
        in_specs=in_specs,
        out_specs=[pl.BlockSpec((tm, n_out), blk), pl.BlockSpec((tm, n_out), blk)],
        out_shape=[
            jax.ShapeDtypeStruct((SEQ, n_out), F32),
            jax.ShapeDtypeStruct((SEQ, n_out), BF16),
        ],
        scratch_shapes=[pltpu.VMEM((k_total, n_out), BF16)],
        compiler_params=_params("arbitrary"),
        name=name,
    )(*lhs_list, w, res, g)


def _up_kernel(f_ref, wg_ref, wu_ref, cwg_ref, cwu_ref, cbg_ref, cbu_ref, act_ref, halo_scr):
    i = pl.program_id(0)
    j = pl.program_id(1)

    @pl.when(i == 0)
    def _():
        halo_scr[j] = jnp.zeros((HALO, 2 * TN_UP), F32)

    gcols, ucols = slice(0, TN_UP), slice(TN_UP, 2 * TN_UP)
    f = f_ref[...]
    y_g = jnp.dot(f, wg_ref[...].astype(BF16), preferred_element_type=F32)
    y_u = jnp.dot(f, wu_ref[...].astype(BF16), preferred_element_type=F32)
    gate, halo_scr[j, :, gcols] = _causal_conv_rows(y_g, halo_scr[j, :, gcols], cwg_ref, cbg_ref,
                                                    FFN_CONV)
    up, halo_scr[j, :, ucols] = _causal_conv_rows(y_u, halo_scr[j, :, ucols], cwu_ref, cbu_ref,
                                                  FFN_CONV)
    act_ref[...] = (gate * _sigmoid(gate) * up).astype(BF16)


def _up_proj(f, w_up, conv_w, conv_b):
    n_t = D_FF // TN_UP
    return pl.pallas_call(
        _up_kernel,
        grid=(SEQ // TM_UP, n_t),
        in_specs=[
            pl.BlockSpec((TM_UP, D_MODEL), lambda i, j: (i, 0)),
            pl.BlockSpec((D_MODEL, TN_UP), lambda i, j: (0, j)),
            pl.BlockSpec((D_MODEL, TN_UP), lambda i, j: (0, j + n_t)),
            pl.BlockSpec((FFN_CONV, TN_UP), lambda i, j: (0, j)),
            pl.BlockSpec((FFN_CONV, TN_UP), lambda i, j: (0, j + n_t)),
            pl.BlockSpec((1, TN_UP), lambda i, j: (0, j)),
            pl.BlockSpec((1, TN_UP), lambda i, j: (0, j + n_t)),
        ],
        out_specs=pl.BlockSpec((TM_UP, TN_UP), lambda i, j: (i, j)),
        out_shape=jax.ShapeDtypeStruct((SEQ, D_FF), BF16),
        scratch_shapes=[pltpu.VMEM((n_t, HALO, 2 * TN_UP), F32)],
        compiler_params=_params("arbitrary", "arbitrary"),
        name="up_proj",
    )(f, w_up, w_up, conv_w, conv_w, conv_b, conv_b)


def _down_ple_kernel(n_cast, act_ref, wdf_ref, h1_ref, gp_ref, wpg_ref, p_ref, wple_ref, gf_ref,
                     o_ref, wd_scr):
    s = pl.program_id(0)

    @pl.when(s < n_cast)
    def _():
        chunk = wdf_ref.shape[0]
        wd_scr[pl.ds(pl.multiple_of(s * chunk, chunk), chunk), :] = wdf_ref[...].astype(BF16)

    @pl.when(s >= n_cast)
    def _():
        emb = jnp.dot(p_ref[...].astype(BF16), wple_ref[...], preferred_element_type=F32)
        h2 = h1_ref[...] + jnp.dot(act_ref[...], wd_scr[...], preferred_element_type=F32)
        n3 = (h2 * _rms_scale(h2) * gp_ref[...]).astype(BF16)
        gate = _sigmoid(jnp.dot(n3, wpg_ref[...], preferred_element_type=F32))
        h3 = h2 + gate * emb
        o_ref[...] = h3 * _rms_scale(h3) * gf_ref[...]


def _down_ple(act, w_down, h1, g_ple, w_pg, p, w_ple, g_final):
    n_cast = D_FF // W_CAST_ROWS
    blk = lambda s: (jnp.maximum(s - n_cast, 0), 0)
    rows = lambda width: pl.BlockSpec((TM_DOWN, width), blk)
    whole = lambda a: pl.BlockSpec(a.shape, lambda s: (0, 0), pipeline_mode=pl.Buffered(1))
    vec = pl.BlockSpec((1, D_MODEL), lambda s: (0, 0))
    wd_chunks = pl.BlockSpec((W_CAST_ROWS, D_MODEL), lambda s: (jnp.minimum(s, n_cast - 1), 0))
    return pl.pallas_call(
        functools.partial(_down_ple_kernel, n_cast),
        grid=(n_cast + SEQ // TM_DOWN,),
        in_specs=[rows(D_FF), wd_chunks, rows(D_MODEL), vec, whole(w_pg), rows(D_PLE),
                  whole(w_ple), vec],
        out_specs=rows(D_MODEL),
        out_shape=jax.ShapeDtypeStruct((SEQ, D_MODEL), F32),
        scratch_shapes=[pltpu.VMEM((D_FF, D_MODEL), BF16)],
        compiler_params=pltpu.CompilerParams(dimension_semantics=("arbitrary",),
                                             vmem_limit_bytes=VMEM_LIMIT_DOWN_BYTES),
        name="down_ple",
    )(act, w_down, h1, g_ple, w_pg, p, w_ple, g_final)


def _layer(h, p_i, norm_mix_g, w_in, ln_a_g, ln_a_b, w_s, b_s, norm_a_g, conv_ssm_w, conv_ssm_b,
           dt_bias, a_log, d_skip, ssm_norm_g, w_out, norm_ffn_g, w_up, conv_ffn_w, conv_ffn_b,
           w_down, norm_ple_g, w_ple_gate, w_ple, norm_final_g):
    row = lambda v: v.reshape(1, -1)
    pad_lanes = lambda v: jnp.pad(v, (0, LANES - v.shape[0])).reshape(1, LANES)

    w_in_t = jnp.swapaxes(w_in, 0, 1)
    w_dt_t = jnp.pad(w_in_t[D_PROJ:], ((0, LANES - N_SSM_HEADS), (0, 0))).astype(BF16)
    proj, dt_raw = _in_proj(h, row(norm_mix_g), w_in_t, w_dt_t, conv_ssm_w, row(conv_ssm_b))

    bs_full = jnp.broadcast_to(b_s[:, :, None], (N_GROUPS_A, CHUNK, D_HEAD_A))
    y_a = _gmlp(proj, row(ln_a_g), row(ln_a_b), w_s, bs_full, row(norm_a_g))

    y_b = _ssd(proj, dt_raw, pad_lanes(dt_bias), pad_lanes(a_log),
               row(jnp.repeat(d_skip, SSM_HEAD_DIM)), row(ssm_norm_g))

    h1, f = _resident_proj_res_norm([y_a, y_b], w_out, h, row(norm_ffn_g),
                                    TM_OUT, "out_proj")
    act = _up_proj(f, w_up, conv_ffn_w, row(conv_ffn_b))
    return _down_ple(act, w_down, h1, row(norm_ple_g), w_ple_gate.astype(BF16), p_i,
                     w_ple.astype(BF16), row(norm_final_g))


def kernel(x, p, norm_mix_g, w_in, ln_a_g, ln_a_b, w_s, b_s, norm_a_g, conv_ssm_w, conv_ssm_b,
           dt_bias, a_log, d_skip, ssm_norm_g, w_out, norm_ffn_g, w_up, conv_ffn_w, conv_ffn_b,
           w_down, norm_ple_g, w_ple_gate, w_ple, norm_final_g):
    assert x.shape == (1, SEQ, D_MODEL) and w_in.shape[0] == 1
    out = _layer(
        x[0], p[0, 0], norm_mix_g[0], w_in[0], ln_a_g[0], ln_a_b[0], w_s[0], b_s[0], norm_a_g[0],
        conv_ssm_w[0], conv_ssm_b[0], dt_bias[0], a_log[0], d_skip[0], ssm_norm_g[0], w_out[0],
        norm_ffn_g[0], w_up[0], conv_ffn_w[0], conv_ffn_b[0], w_down[0], norm_ple_g[0],
        w_ple_gate[0], w_ple[0], norm_final_g)
    return out[None]
```

```python
import functools
import math

import jax
import jax.numpy as jnp
from jax import lax
from jax.experimental import pallas as pl
from jax.experimental.pallas import tpu as pltpu

F32 = jnp.float32
BF16 = jnp.bfloat16

D_MODEL = 2048
SEQ = 8192
D_A = 2048
CHUNK = 128
N_GROUPS_A = 16
D_HEAD_A = 128
D_SSM = 2048
SSM_HEAD_DIM = 64
N_SSM_HEADS = 32
N_SSM_GROUPS = 8
HEADS_PER_GROUP = 4
D_STATE = 128
SSM_CONV = 4
D_XBC = 4096
D_PROJ = 10240
D_FF = 5632
FFN_CONV = 3
D_PLE = 256
EPS = 1e-6
LOG2E = 1.0 / math.log(2.0)

LANES = 128
SUBLANES = 8
HALO = SUBLANES
VMEM_LIMIT_BYTES = 56 * 1024 * 1024
VMEM_LIMIT_DOWN_BYTES = 60 * 1024 * 1024

TM_IN = 1024
TN_IN = 1024
N_UV_TILES = (2 * D_A) // TN_IN
N_Z_END = (2 * D_A + D_SSM) // TN_IN
SUB_M = 256
R_MIX = 512
TM_OUT = 512
W_CAST_ROWS = 512
TM_UP = 1024
TN_UP = 512
TM_DOWN = 256


def _params(*sem):
    return pltpu.CompilerParams(dimension_semantics=sem, vmem_limit_bytes=VMEM_LIMIT_BYTES)


def _sigmoid(x):
    return 1.0 / (1.0 + jnp.exp(-x))


def _gelu_tanh(x):
    inner = x * (1.0 + 0.044715 * (x * x))
    return x / (1.0 + jnp.exp2(inner * (-2.0 * math.sqrt(2.0 / math.pi) * LOG2E)))


def _rms_scale(x):
    return lax.rsqrt(jnp.mean(x * x, axis=-1, keepdims=True) + EPS)


def _causal_conv_rows(y, hist, cw_ref, cb_ref, width):
    full = jnp.concatenate([hist, y], axis=0)
    acc = cb_ref[...] + cw_ref[width - 1:width, :] * y
    for k in range(width - 1):
        acc = acc + cw_ref[k:k + 1, :] * pltpu.roll(full, width - 1 - k, axis=0)[HALO:, :]
    return acc, y[y.shape[0] - HALO:, :]


def _dot_nt(a, bt):
    return lax.dot_general(a, bt, (((1,), (1,)), ((), ())), preferred_element_type=F32)


def _in_proj_kernel(x_ref, g_ref, wt_ref, wdt_ref, cw_ref, cb_ref, proj_ref, dt_ref,
                    a_scr, halo_scr):
    i = pl.program_id(0)
    j = pl.program_id(1)
    t = jnp.maximum(j - N_Z_END, 0)

    @pl.when(j == 0)
    def _():
        x = x_ref[...]
        a = (x * _rms_scale(x) * g_ref[...]).astype(BF16)
        a_scr[...] = a
        dt_ref[...] = _dot_nt(a, wdt_ref[...])

    @pl.when((i == 0) & (j >= N_Z_END))
    def _():
        halo_scr[t] = jnp.zeros((HALO, TN_IN), F32)

    @pl.when(j < N_UV_TILES)
    def _():
        y = _dot_nt(a_scr[...], wt_ref[...].astype(BF16))
        proj_ref[...] = _gelu_tanh(y).astype(BF16)

    @pl.when((j >= N_UV_TILES) & (j < N_Z_END))
    def _():
        y = _dot_nt(a_scr[...], wt_ref[...].astype(BF16))
        proj_ref[...] = (y * _sigmoid(y)).astype(BF16)

    @pl.when(j >= N_Z_END)
    def _():
        y = _dot_nt(a_scr[...], wt_ref[...].astype(BF16))
        acc, halo_scr[t] = _causal_conv_rows(y, halo_scr[t], cw_ref, cb_ref, SSM_CONV)
        proj_ref[...] = (acc * _sigmoid(acc)).astype(BF16)


def _in_proj(x, g, w, wdt, conv_w, conv_b):
    xbc_tile = lambda i, j: (0, jnp.maximum(j - N_Z_END, 0))
    return pl.pallas_call(
        _in_proj_kernel,
        grid=(SEQ // TM_IN, D_PROJ // TN_IN),
        in_specs=[
            pl.BlockSpec((TM_IN, D_MODEL), lambda i, j: (i, 0)),
            pl.BlockSpec((1, D_MODEL), lambda i, j: (0, 0)),
            pl.BlockSpec((TN_IN, D_MODEL), lambda i, j: (j, 0)),
            pl.BlockSpec((LANES, D_MODEL), lambda i, j: (0, 0)),
            pl.BlockSpec((SSM_CONV, TN_IN), xbc_tile),
            pl.BlockSpec((1, TN_IN), xbc_tile),
        ],
        out_specs=[
            pl.BlockSpec((TM_IN, TN_IN), lambda i, j: (i, j)),
            pl.BlockSpec((TM_IN, LANES), lambda i, j: (i, 0)),
        ],
        out_shape=[
            jax.ShapeDtypeStruct((SEQ, D_PROJ), BF16),
            jax.ShapeDtypeStruct((SEQ, LANES), F32),
        ],
        scratch_shapes=[
            pltpu.VMEM((TM_IN, D_MODEL), BF16),
            pltpu.VMEM((D_XBC // TN_IN, HALO, TN_IN), F32),
        ],
        compiler_params=_params("arbitrary", "arbitrary"),
        name="in_proj",
    )(x, g, w, wdt, conv_w, conv_b)


def _gmlp_chunk_fn(u_ref, v_ref, lng_ref, lnb_ref, ws_ref, bs_ref, ng_ref, ya_ref, wsm_scr, acc_scr):
    @pl.when(pl.program_id(0) == 0)
    def _():
        row = lax.broadcasted_iota(jnp.int32, (CHUNK, CHUNK), 0)
        col = lax.broadcasted_iota(jnp.int32, (CHUNK, CHUNK), 1)
        for g in range(N_GROUPS_A):
            wsm_scr[g] = jnp.where(row >= col, ws_ref[g], 0.0).astype(BF16)

    gi = lax.broadcasted_iota(jnp.int32, (2 * D_HEAD_A, 2 * D_HEAD_A), 0) // D_HEAD_A
    gj = lax.broadcasted_iota(jnp.int32, (2 * D_HEAD_A, 2 * D_HEAD_A), 1) // D_HEAD_A
    mean_m = jnp.where(gi == gj, 1.0 / D_HEAD_A, 0.0).astype(BF16)

    def group_mean(x):
        return jnp.dot(x, mean_m, preferred_element_type=F32)

    def chunk_body(rows):
        pairs = [slice(gp * 2 * D_HEAD_A, (gp + 1) * 2 * D_HEAD_A) for gp in range(N_GROUPS_A // 2)]
        vbs = [v_ref[rows, cols2] for cols2 in pairs]
        vcs = [vb.astype(F32) - group_mean(vb) for vb in vbs]
        vars_ = [group_mean((vc * vc).astype(BF16)) for vc in vcs]
        vns = [(vc * lax.rsqrt(var + EPS) * lng_ref[:, cols2] + lnb_ref[:, cols2]).astype(BF16)
               for vc, var, cols2 in zip(vcs, vars_, pairs)]
        ssq = jnp.zeros((CHUNK, D_HEAD_A), F32)
        for g in range(N_GROUPS_A):
            cols = slice(g * D_HEAD_A, (g + 1) * D_HEAD_A)
            vn = vns[g // 2][:, (g % 2) * D_HEAD_A:(g % 2 + 1) * D_HEAD_A]
            sg = jnp.dot(wsm_scr[g], vn, preferred_element_type=F32) + bs_ref[g]
            ya = u_ref[rows, cols].astype(F32) * sg
            ssq = ssq + ya * ya
            acc_scr[rows, cols] = ya
        inv = lax.rsqrt(jnp.sum(ssq, axis=-1, keepdims=True) * (1.0 / D_A) + EPS)
        ya_ref[rows, :] = (acc_scr[rows, :] * inv * ng_ref[...]).astype(BF16)

    return chunk_body


def _ssd_chunk_fn(zs_ref, xs_ref, bc_ref, dt_ref, dtb_ref, alog_ref, dskip_ref, ng_ref, yb_ref,
                  h_scr, cs2_scr, srct_scr, w2t_scr):
    @pl.when(pl.program_id(0) == 0)
    def _():
        h_scr[...] = jnp.zeros((D_STATE, D_SSM), F32)

    row = lax.broadcasted_iota(jnp.int32, (CHUNK, CHUNK), 0)
    lane = lax.broadcasted_iota(jnp.int32, (CHUNK, CHUNK), 1)
    tril = row >= lane
    tri_b = jnp.where(tril, 1.0, 0.0).astype(BF16)
    left = lane < SSM_HEAD_DIM
    mask_l = jnp.where(left, 1.0, 0.0).astype(BF16)
    mask_r = jnp.where(left, 0.0, 1.0).astype(BF16)
    a_neg = -jnp.exp(alog_ref[...])

    dtv = dt_ref[...] + dtb_ref[...]
    dt_all = jnp.maximum(dtv, 0.0) + jnp.log1p(jnp.exp(-jnp.abs(dtv)))
    for c in range(R_MIX // CHUNK):
        crows = slice(c * CHUNK, (c + 1) * CHUNK)
        dt = dt_all[crows, :]
        adt = dt * a_neg
        hi = adt.astype(BF16)
        r1 = adt - hi.astype(F32)
        mid = r1.astype(BF16)
        lo = (r1 - mid.astype(F32)).astype(BF16)
        cs = (jnp.dot(tri_b, hi, preferred_element_type=F32)
              + jnp.dot(tri_b, mid, preferred_element_type=F32)
              + jnp.dot(tri_b, lo, preferred_element_type=F32))
        cs2 = cs * LOG2E
        cs2_scr[crows, :] = cs2
        srct_scr[crows, :] = (cs2 - jnp.log2(dt)).T
        w2t_scr[crows, :] = (dt * jnp.exp2(cs2[CHUNK - 1:CHUNK, :] - cs2)).T

    def chunk_body(rows):
        cs2 = cs2_scr[rows, :]
        src_t = srct_scr[rows, :]
        w2_t = w2t_scr[rows, :]

        c_fs, bt_fs, scores_all = [], [], []
        for g in range(N_SSM_GROUPS):
            b_g = bc_ref[rows, g * D_STATE:(g + 1) * D_STATE]
            c_g = bc_ref[rows, N_SSM_GROUPS * D_STATE + g * D_STATE:
                         N_SSM_GROUPS * D_STATE + (g + 1) * D_STATE]
            bt_f = b_g.astype(F32).T
            c_fs.append(c_g.astype(F32))
            bt_fs.append(bt_f)
            scores_all.append(jnp.dot(c_g, bt_f.astype(BF16), preferred_element_type=F32))

        for g in range(N_SSM_GROUPS):
            c_f, bt_f, scores = c_fs[g], bt_fs[g], scores_all[g]
            ys = []
            ssq = jnp.zeros((CHUNK, 1), F32)
            for jp in range(2):
                pcols = slice((2 * g + jp) * LANES, (2 * g + jp + 1) * LANES)
                xp = xs_ref[rows, pcols]
                hp = h_scr[:, pcols]
                hpb = hp.astype(BF16)
                lhs_y, lhs_s, decay = [], [], []
                for hh in range(2):
                    h = g * HEADS_PER_GROUP + 2 * jp + hh
                    colv = jnp.sum(jnp.where(lane == h, cs2, 0.0), axis=-1, keepdims=True)
                    lmat = jnp.exp2(jnp.where(tril, colv - src_t[h:h + 1, :], -jnp.inf))
                    ecol = jnp.exp2(colv)
                    lhs_y += [(scores * lmat).astype(BF16), (c_f * ecol).astype(BF16)]
                    lhs_s.append((bt_f * w2_t[h:h + 1, :]).astype(BF16))
                    decay.append(jnp.exp2(colv[CHUNK - 1:CHUNK, :]))
                x_l, x_r = xp * mask_l, xp * mask_r
                yd = jnp.dot(jnp.concatenate(lhs_y, axis=1),
                             jnp.concatenate([x_l, hpb * mask_l, x_r, hpb * mask_r], axis=0),
                             preferred_element_type=F32)
                states = jnp.dot(jnp.concatenate(lhs_s, axis=1),
                                 jnp.concatenate([x_l, x_r], axis=0),
                                 preferred_element_type=F32)
                h_scr[:, pcols] = jnp.where(left, decay[0], decay[1]) * hp + states

                y = yd + dskip_ref[:, pcols] * xp.astype(F32)
                y = y * zs_ref[rows, pcols].astype(F32)
                ssq = ssq + jnp.sum(y * y, axis=-1, keepdims=True)
                ys.append((pcols, y))
            inv = lax.rsqrt(ssq * (1.0 / (2 * LANES)) + EPS)
            for pcols, y in ys:
                yb_ref[rows, pcols] = (y * inv * ng_ref[:, pcols]).astype(BF16)

    return chunk_body


N_GMLP_IN, N_SSD_IN = 7, 8


def _mixers_kernel(*refs):
    gmlp_in, refs = refs[:N_GMLP_IN], refs[N_GMLP_IN:]
    ssd_in, refs = refs[:N_SSD_IN], refs[N_SSD_IN:]
    ya_ref, yb_ref, wsm_scr, acc_scr, h_scr, cs2_scr, srct_scr, w2t_scr = refs
    gmlp_chunk = _gmlp_chunk_fn(*gmlp_in, ya_ref, wsm_scr, acc_scr)
    ssd_chunk = _ssd_chunk_fn(*ssd_in, yb_ref, h_scr, cs2_scr, srct_scr, w2t_scr)

    def body(c, carry):
        rows = pl.ds(pl.multiple_of(c * CHUNK, CHUNK), CHUNK)
        gmlp_chunk(rows)
        ssd_chunk(rows)
        return carry

    lax.fori_loop(0, R_MIX // CHUNK, body, 0)


def _mixers(proj, dt_raw, ln_g, ln_b, w_s, bs_full, norm_a_g, dt_bias, a_log, d_skip_e, norm_b_g):
    blk = lambda col: pl.BlockSpec((R_MIX, D_A), lambda i: (i, col))
    full = lambda shape: pl.BlockSpec(shape, lambda i: (0,) * len(shape))
    out = pl.BlockSpec((R_MIX, D_A), lambda i: (i, 0))
    return pl.pallas_call(
        _mixers_kernel,
        grid=(SEQ // R_MIX,),
        in_specs=[
            blk(0), blk(1), full((1, D_A)), full((1, D_A)), full((N_GROUPS_A, CHUNK, CHUNK)),
            full((N_GROUPS_A, CHUNK, D_HEAD_A)), full((1, D_A)),
            blk(2), blk(3), blk(4), pl.BlockSpec((R_MIX, LANES), lambda i: (i, 0)),
            full((1, LANES)), full((1, LANES)), full((1, D_SSM)), full((1, D_SSM)),
        ],
        out_specs=[out, out],
        out_shape=[jax.ShapeDtypeStruct((SEQ, D_A), BF16), jax.ShapeDtypeStruct((SEQ, D_SSM), BF16)],
        scratch_shapes=[
            pltpu.VMEM((N_GROUPS_A, CHUNK, CHUNK), BF16),
            pltpu.VMEM((R_MIX, D_A), F32),
            pltpu.VMEM((D_STATE, D_SSM), F32),
        ] + [pltpu.VMEM((R_MIX, LANES), F32)] * 3,
        compiler_params=_params("arbitrary"),
        name="mixers",
    )(proj, proj, ln_g, ln_b, w_s, bs_full, norm_a_g,
      proj, proj, proj, dt_raw, dt_bias, a_log, d_skip_e, norm_b_g)


def _resident_proj_res_norm_kernel(n_lhs, tm, n_cast, *refs):
    lhs_refs = refs[:n_lhs]
    wf_ref, res_ref, g_ref, h_ref, n_ref, w_scr = refs[n_lhs:]
    s = pl.program_id(0)

    @pl.when(s < n_cast)
    def _():
        chunk = wf_ref.shape[0]
        w_scr[pl.ds(pl.multiple_of(s * chunk, chunk), chunk), :] = wf_ref[...].astype(BF16)

    @pl.when(s >= n_cast)
    def _():
        for r in range(tm // SUB_M):
            rows = slice(r * SUB_M, (r + 1) * SUB_M)
            h = res_ref[rows, :]
            k0 = 0
            for lhs in lhs_refs:
                k1 = k0 + lhs.shape[1]
                h = h + jnp.dot(lhs[rows, :], w_scr[k0:k1, :], preferred_element_type=F32)
                k0 = k1
            h_ref[rows, :] = h
            n_ref[rows, :] = (h * _rms_scale(h) * g_ref[...]).astype(BF16)


def _resident_proj_res_norm(lhs_list, w, res, g, tm, name):
    k_total, n_out = w.shape
    n_cast = k_total // W_CAST_ROWS
    blk = lambda s: (jnp.maximum(s - n_cast, 0), 0)
    in_specs = [pl.BlockSpec((tm, l.shape[1]), blk) for l in lhs_list]
    in_specs += [
        pl.BlockSpec((W_CAST_ROWS, n_out), lambda s: (jnp.minimum(s, n_cast - 1), 0)),
        pl.BlockSpec((tm, n_out), blk),
        pl.BlockSpec((1, n_out), lambda s: (0, 0)),
    ]
    return pl.pallas_call(
        functools.partial(_resident_proj_res_norm_kernel, len(lhs_list), tm, n_cast),
        grid=(n_cast + SEQ // tm,),
        in_specs=in_specs,
        out_specs=[pl.BlockSpec((tm, n_out), blk), pl.BlockSpec((tm, n_out), blk)],
        out_shape=[
            jax.ShapeDtypeStruct((SEQ, n_out), F32),
            jax.ShapeDtypeStruct((SEQ, n_out), BF16),
        ],
        scratch_shapes=[pltpu.VMEM((k_total, n_out), BF16)],
        compiler_params=_params("arbitrary"),
        name=name,
    )(*lhs_list, w, res, g)


def _up_kernel(f_ref, wg_ref, wu_ref, cwg_ref, cwu_ref, cbg_ref, cbu_ref, act_ref, halo_scr):
    i = pl.program_id(0)
    j = pl.program_id(1)

    @pl.when(i == 0)
    def _():
        halo_scr[j] = jnp.zeros((HALO, 2 * TN_UP), F32)

    gcols, ucols = slice(0, TN_UP), slice(TN_UP, 2 * TN_UP)
    f = f_ref[...]
    y_g = jnp.dot(f, wg_ref[...].astype(BF16), preferred_element_type=F32)
    y_u = jnp.dot(f, wu_ref[...].astype(BF16), preferred_element_type=F32)
    gate, halo_scr[j, :, gcols] = _causal_conv_rows(y_g, halo_scr[j, :, gcols], cwg_ref, cbg_ref,
                                                    FFN_CONV)
    up, halo_scr[j, :, ucols] = _causal_conv_rows(y_u, halo_scr[j, :, ucols], cwu_ref, cbu_ref,
                                                  FFN_CONV)
    act_ref[...] = (gate * _sigmoid(gate) * up).astype(BF16)


def _up_proj(f, w_up, conv_w, conv_b):
    n_t = D_FF // TN_UP
    return pl.pallas_call(
        _up_kernel,
        grid=(SEQ // TM_UP, n_t),
        in_specs=[
            pl.BlockSpec((TM_UP, D_MODEL), lambda i, j: (i, 0)),
            pl.BlockSpec((D_MODEL, TN_UP), lambda i, j: (0, j)),
            pl.BlockSpec((D_MODEL, TN_UP), lambda i, j: (0, j + n_t)),
            pl.BlockSpec((FFN_CONV, TN_UP), lambda i, j: (0, j)),
            pl.BlockSpec((FFN_CONV, TN_UP), lambda i, j: (0, j + n_t)),
            pl.BlockSpec((1, TN_UP), lambda i, j: (0, j)),
            pl.BlockSpec((1, TN_UP), lambda i, j: (0, j + n_t)),
        ],
        out_specs=pl.BlockSpec((TM_UP, TN_UP), lambda i, j: (i, j)),
        out_shape=jax.ShapeDtypeStruct((SEQ, D_FF), BF16),
        scratch_shapes=[pltpu.VMEM((n_t, HALO, 2 * TN_UP), F32)],
        compiler_params=_params("arbitrary", "arbitrary"),
        name="up_proj",
    )(f, w_up, w_up, conv_w, conv_w, conv_b, conv_b)


def _down_ple_kernel(n_cast, act_ref, wdf_ref, h1_ref, gp_ref, wpg_ref, p_ref, wple_ref, gf_ref,
                     o_ref, wd_scr):
    s = pl.program_id(0)

    @pl.when(s < n_cast)
    def _():
        chunk = wdf_ref.shape[0]
        wd_scr[pl.ds(pl.multiple_of(s * chunk, chunk), chunk), :] = wdf_ref[...].astype(BF16)

    @pl.when(s >= n_cast)
    def _():
        emb = jnp.dot(p_ref[...].astype(BF16), wple_ref[...], preferred_element_type=F32)
        h2 = h1_ref[...] + jnp.dot(act_ref[...], wd_scr[...], preferred_element_type=F32)
        n3 = (h2 * _rms_scale(h2) * gp_ref[...]).astype(BF16)
        gate = _sigmoid(jnp.dot(n3, wpg_ref[...], preferred_element_type=F32))
        h3 = h2 + gate * emb
        o_ref[...] = h3 * _rms_scale(h3) * gf_ref[...]


def _down_ple(act, w_down, h1, g_ple, w_pg, p, w_ple, g_final):
    n_cast = D_FF // W_CAST_ROWS
    blk = lambda s: (jnp.maximum(s - n_cast, 0), 0)
    rows = lambda width: pl.BlockSpec((TM_DOWN, width), blk)
    whole = lambda a: pl.BlockSpec(a.shape, lambda s: (0, 0), pipeline_mode=pl.Buffered(1))
    vec = pl.BlockSpec((1, D_MODEL), lambda s: (0, 0))
    wd_chunks = pl.BlockSpec((W_CAST_ROWS, D_MODEL), lambda s: (jnp.minimum(s, n_cast - 1), 0))
    return pl.pallas_call(
        functools.partial(_down_ple_kernel, n_cast),
        grid=(n_cast + SEQ // TM_DOWN,),
        in_specs=[rows(D_FF), wd_chunks, rows(D_MODEL), vec, whole(w_pg), rows(D_PLE),
                  whole(w_ple), vec],
        out_specs=rows(D_MODEL),
        out_shape=jax.ShapeDtypeStruct((SEQ, D_MODEL), F32),
        scratch_shapes=[pltpu.VMEM((D_FF, D_MODEL), BF16)],
        compiler_params=pltpu.CompilerParams(dimension_semantics=("arbitrary",),
                                             vmem_limit_bytes=VMEM_LIMIT_DOWN_BYTES),
        name="down_ple",
    )(act, w_down, h1, g_ple, w_pg, p, w_ple, g_final)


def _layer(h, p_i, norm_mix_g, w_in, ln_a_g, ln_a_b, w_s, b_s, norm_a_g, conv_ssm_w, conv_ssm_b,
           dt_bias, a_log, d_skip, ssm_norm_g, w_out, norm_ffn_g, w_up, conv_ffn_w, conv_ffn_b,
           w_down, norm_ple_g, w_ple_gate, w_ple, norm_final_g):
    row = lambda v: v.reshape(1, -1)
    pad_lanes = lambda v: jnp.pad(v, (0, LANES - v.shape[0])).reshape(1, LANES)

    w_in_t = jnp.swapaxes(w_in, 0, 1)
    w_dt_t = jnp.pad(w_in_t[D_PROJ:], ((0, LANES - N_SSM_HEADS), (0, 0))).astype(BF16)
    proj, dt_raw = _in_proj(h, row(norm_mix_g), w_in_t, w_dt_t, conv_ssm_w, row(conv_ssm_b))

    bs_full = jnp.broadcast_to(b_s[:, :, None], (N_GROUPS_A, CHUNK, D_HEAD_A))
    y_a, y_b = _mixers(proj, dt_raw, row(ln_a_g), row(ln_a_b), w_s, bs_full, row(norm_a_g),
                       pad_lanes(dt_bias), pad_lanes(a_log),
                       row(jnp.repeat(d_skip, SSM_HEAD_DIM)), row(ssm_norm_g))

    h1, f = _resident_proj_res_norm([y_a, y_b], w_out, h, row(norm_ffn_g),
                                    TM_OUT, "out_proj")
    act = _up_proj(f, w_up, conv_ffn_w, row(conv_ffn_b))
    return _down_ple(act, w_down, h1, row(norm_ple_g), w_ple_gate.astype(BF16), p_i,
                     w_ple.astype(BF16), row(norm_final_g))


def kernel(x, p, norm_mix_g, w_in, ln_a_g, ln_a_b, w_s, b_s, norm_a_g, conv_ssm_w, conv_ssm_b,
           dt_bias, a_log, d_skip, ssm_norm_g, w_out, norm_ffn_g, w_up, conv_ffn_w, conv_ffn_b,
           w_down, norm_ple_g, w_ple_gate, w_ple, norm_final_g):
    assert x.shape == (1, SEQ, D_MODEL) and w_in.shape[0] == 1
    out = _layer(
        x[0], p[0, 0], norm_mix_g[0], w_in[0], ln_a_g[0], ln_a_b[0], w_s[0], b_s[0], norm_a_g[0],
        conv_ssm_w[0], conv_ssm_b[0], dt_bias[0], a_log[0], d_skip[0], ssm_norm_g[0], w_out[0],
        norm_ffn_g[0], w_up[0], conv_ffn_w[0], conv_ffn_b[0], w_down[0], norm_ple_g[0],
        w_ple_gate[0], w_ple[0], norm_final_g)
    return out[None]
```
